```python
import math
import jax, jax.numpy as jnp
from jax import lax
import numpy as np


D_MODEL = 1024
BATCH = 16
SEQ = 4096
DEPTH = 4
DEC_BATCH = 16
DEC_SEQ = 2048
PAST_LEN = 128

N_META = 16
HEAD_DIM = 64
BLOCK = 128
WINDOW = 128
ROPE_THETA = 10000.0
EPS = 1e-6
NEG_INF = -1e30

A_WIDTH = D_MODEL // 2
A_HEADS = A_WIDTH // HEAD_DIM
A_KV_HEADS = A_HEADS // 4
A_GROUP = A_HEADS // A_KV_HEADS
B_WIDTH = D_MODEL - A_WIDTH
B_VDIM = 2 * HEAD_DIM
B_HEADS = B_WIDTH // B_VDIM
MIX_WIDTH = A_WIDTH + B_WIDTH

A_Q = A_HEADS * HEAD_DIM
A_KV = A_KV_HEADS * HEAD_DIM
B_QK = 2 * B_HEADS * HEAD_DIM
IN_SIZES = (A_Q, A_KV, A_KV, A_WIDTH, B_QK, B_QK, B_WIDTH, B_WIDTH)
IN_WIDTH = A_Q + 2 * A_KV + A_WIDTH + 2 * B_QK + 2 * B_WIDTH

kernel_name = 'hymba_style_window_gqa_diff_attn_encoder'


def rmsnorm(x, g):
    xf = x.astype(jnp.float32)
    y = xf * lax.rsqrt(jnp.mean(xf * xf, axis=-1, keepdims=True) + EPS) * g.astype(jnp.float32)
    return y.astype(x.dtype)


def rope_tables(length):
    inv_freq = 1.0 / (ROPE_THETA ** (jnp.arange(0, HEAD_DIM, 2, dtype=jnp.float32) / HEAD_DIM))
    ang = jnp.arange(length, dtype=jnp.float32)[:, None] * inv_freq[None, :]
    ang = jnp.concatenate([ang, ang], axis=-1)
    return jnp.cos(ang), jnp.sin(ang)


def apply_rope(x, cos, sin):
    shp = (x.shape[1],) + (1,) * (x.ndim - 3) + (HEAD_DIM,)
    c, s = cos.reshape(shp), sin.reshape(shp)
    xf = x.astype(jnp.float32)
    half = HEAD_DIM // 2
    rot = jnp.concatenate([-xf[..., half:], xf[..., :half]], axis=-1)
    return (xf * c + rot * s).astype(x.dtype)


def window_gqa_attention(q, k, v, sink):
    bsz, length = q.shape[0], q.shape[1]
    s = length - N_META
    nb = s // BLOCK
    scale = HEAD_DIM ** -0.5
    q = q.reshape(bsz, length, A_KV_HEADS, A_GROUP, HEAD_DIM)
    qm, qr = q[:, :N_META], q[:, N_META:]
    km, kr = k[:, :N_META], k[:, N_META:]
    vm, vr = v[:, :N_META], v[:, N_META:]
    sink_f = sink.astype(jnp.float32).reshape(A_KV_HEADS, A_GROUP)

    qb = qr.reshape(bsz, nb, BLOCK, A_KV_HEADS, A_GROUP, HEAD_DIM)

    def band(t):
        tp = jnp.pad(t, ((0, 0), (BLOCK, BLOCK), (0, 0), (0, 0)))
        tp = tp.reshape(bsz, nb + 2, BLOCK, A_KV_HEADS, HEAD_DIM)
        return jnp.concatenate([tp[:, :-2], tp[:, 1:-1], tp[:, 2:]], axis=2)

    kb, vb = band(kr), band(vr)
    qi = jnp.arange(nb)[:, None] * BLOCK + jnp.arange(BLOCK)[None, :]
    kj = (jnp.arange(nb)[:, None] - 1) * BLOCK + jnp.arange(3 * BLOCK)[None, :]
    rel = kj[:, None, :] - qi[:, :, None]
    valid = (jnp.abs(rel) <= WINDOW) & (kj[:, None, :] >= 0) & (kj[:, None, :] < s)

    s_meta = jnp.einsum('bnqkgd,bmkd->bnkgqm', qb, km, preferred_element_type=jnp.float32) * scale
    s_band = jnp.einsum('bnqkgd,bnukd->bnkgqu', qb, kb, preferred_element_type=jnp.float32) * scale
    s_band = jnp.where(valid[None, :, None, None], s_band, NEG_INF)
    s_sink = jnp.broadcast_to(sink_f[None, None, :, :, None, None], s_meta.shape[:-1] + (1,))
    p = jax.nn.softmax(jnp.concatenate([s_meta, s_band, s_sink], axis=-1), axis=-1).astype(v.dtype)
    o = (jnp.einsum('bnkgqm,bmkd->bnqkgd', p[..., :N_META], vm)
         + jnp.einsum('bnkgqu,bnukd->bnqkgd', p[..., N_META:N_META + 3 * BLOCK], vb))
    o_real = o.reshape(bsz, s, A_WIDTH)

    kr0, vr0 = kr[:, :BLOCK], vr[:, :BLOCK]
    mpos = jnp.arange(N_META)
    rpos = N_META + jnp.arange(BLOCK)
    mvalid = (rpos[None, :] - mpos[:, None]) <= WINDOW
    sm_meta = jnp.einsum('bmkgd,bnkd->bkgmn', qm, km, preferred_element_type=jnp.float32) * scale
    sm_real = jnp.einsum('bmkgd,bukd->bkgmu', qm, kr0, preferred_element_type=jnp.float32) * scale
    sm_real = jnp.where(mvalid[None, None, None], sm_real, NEG_INF)
    sm_sink = jnp.broadcast_to(sink_f[None, :, :, None, None], sm_meta.shape[:-1] + (1,))
    pm = jax.nn.softmax(jnp.concatenate([sm_meta, sm_real, sm_sink], axis=-1), axis=-1).astype(v.dtype)
    om = (jnp.einsum('bkgmn,bnkd->bmkgd', pm[..., :N_META], vm)
          + jnp.einsum('bkgmu,bukd->bmkgd', pm[..., N_META:N_META + BLOCK], vr0))
    o_meta = om.reshape(bsz, N_META, A_WIDTH)
    return jnp.concatenate([o_meta, o_real], axis=1)


def diff_attention(q, k, v, lam, lambda_init, subln_g):
    bsz, length = q.shape[0], q.shape[1]
    s = length - N_META
    nb = s // BLOCK
    scale = HEAD_DIM ** -0.5

    def attend(qblk):
        sc = jnp.einsum('bqhcd,bkhcd->bhcqk', qblk, k, preferred_element_type=jnp.float32) * scale
        p = jax.nn.softmax(sc, axis=-1)
        a = p[:, :, 0] - lam * p[:, :, 1]
        return jnp.einsum('bhqk,bkhe->bqhe', a.astype(v.dtype), v)

    o_meta = attend(q[:, :N_META])
    qb = q[:, N_META:].reshape(bsz, nb, BLOCK, B_HEADS, 2, HEAD_DIM).swapaxes(0, 1)
    o_real = lax.map(attend, qb).swapaxes(0, 1).reshape(bsz, s, B_HEADS, B_VDIM)
    o = jnp.concatenate([o_meta, o_real], axis=1)
    o = rmsnorm(o, subln_g) * (1.0 - lambda_init)
    return o.reshape(bsz, length, B_WIDTH)


def layer(x, cos, sin, w_in, w_out, pre_g, post_g, sink, lq1, lk1, lq2, lk2, subln_g, lambda_init):
    bsz, length, _ = x.shape
    h = rmsnorm(x, pre_g)
    proj = jnp.einsum('bld,de->ble', h, w_in)
    parts = []
    off = 0
    for size in IN_SIZES:
        parts.append(proj[..., off:off + size])
        off += size
    aq, ak, av, ag, bq, bk, bv, bg = parts

    aq = apply_rope(aq.reshape(bsz, length, A_HEADS, HEAD_DIM), cos, sin)
    ak = apply_rope(ak.reshape(bsz, length, A_KV_HEADS, HEAD_DIM), cos, sin)
    av = av.reshape(bsz, length, A_KV_HEADS, HEAD_DIM)
    oa = window_gqa_attention(aq, ak, av, sink) * jax.nn.silu(ag)

    f32 = jnp.float32
    lam = (jnp.exp(jnp.sum(lq1.astype(f32) * lk1.astype(f32)))
           - jnp.exp(jnp.sum(lq2.astype(f32) * lk2.astype(f32))) + lambda_init)
    bq = apply_rope(bq.reshape(bsz, length, B_HEADS, 2, HEAD_DIM), cos, sin)
    bk = apply_rope(bk.reshape(bsz, length, B_HEADS, 2, HEAD_DIM), cos, sin)
    bv = bv.reshape(bsz, length, B_HEADS, B_VDIM)
    ob = diff_attention(bq, bk, bv, lam, lambda_init, subln_g) * jax.nn.silu(bg)

    o = jnp.einsum('ble,ed->bld', jnp.concatenate([oa, ob], axis=-1), w_out)
    return x + rmsnorm(o, post_g)


def encode(x, meta_tokens, w_in, w_out, pre_norm_g, post_norm_g, sink_logits,
           lambda_q1, lambda_k1, lambda_q2, lambda_k2, subln_g):
    bsz, s, _ = x.shape
    meta = jnp.broadcast_to(meta_tokens[None].astype(x.dtype), (bsz, N_META, D_MODEL))
    h = jnp.concatenate([meta, x], axis=1)
    cos, sin = rope_tables(N_META + s)
    for l in range(DEPTH):
        lambda_init = 0.8 - 0.6 * math.exp(-0.3 * l)
        h = layer(h, cos, sin, w_in[l], w_out[l], pre_norm_g[l], post_norm_g[l], sink_logits[l],
                  lambda_q1[l], lambda_k1[l], lambda_q2[l], lambda_k2[l], subln_g[l], lambda_init)
    return h[:, N_META:]


def setup_inputs(seed: int = 0) -> dict:
    key = jax.random.key(seed)
    ks = jax.random.split(key, 14)
    f32 = jnp.float32
    return {
        'x_prompt': jax.random.normal(ks[0], (BATCH, SEQ, D_MODEL), f32),
        'x_sample': jax.random.normal(ks[1], (DEC_BATCH, DEC_SEQ, D_MODEL), f32),
        'meta_tokens': jax.random.normal(ks[2], (N_META, D_MODEL), f32),
        'w_in': jax.random.normal(ks[3], (DEPTH, D_MODEL, IN_WIDTH), f32) * D_MODEL ** -0.5,
        'w_out': jax.random.normal(ks[4], (DEPTH, MIX_WIDTH, D_MODEL), f32) * MIX_WIDTH ** -0.5,
        'pre_norm_g': 1.0 + 0.02 * jax.random.normal(ks[5], (DEPTH, D_MODEL), f32),
        'post_norm_g': 1.0 + 0.02 * jax.random.normal(ks[6], (DEPTH, D_MODEL), f32),
        'sink_logits': 0.5 * jax.random.normal(ks[7], (DEPTH, A_HEADS), f32),
        'lambda_q1': 0.1 * jax.random.normal(ks[8], (DEPTH, HEAD_DIM), f32),
        'lambda_k1': 0.1 * jax.random.normal(ks[9], (DEPTH, HEAD_DIM), f32),
        'lambda_q2': 0.1 * jax.random.normal(ks[10], (DEPTH, HEAD_DIM), f32),
        'lambda_k2': 0.1 * jax.random.normal(ks[11], (DEPTH, HEAD_DIM), f32),
        'subln_g': 1.0 + 0.02 * jax.random.normal(ks[12], (DEPTH, B_VDIM), f32),
    }


def reference(x_prompt, x_sample, meta_tokens, w_in, w_out, pre_norm_g, post_norm_g, sink_logits,
              lambda_q1, lambda_k1, lambda_q2, lambda_k2, subln_g):
    y_prompt = encode(x_prompt, meta_tokens, w_in, w_out, pre_norm_g, post_norm_g, sink_logits,
                      lambda_q1, lambda_k1, lambda_q2, lambda_k2, subln_g)
    y_sample = encode(x_sample, meta_tokens, w_in, w_out, pre_norm_g, post_norm_g, sink_logits,
                      lambda_q1, lambda_k1, lambda_q2, lambda_k2, subln_g)
    return (y_prompt, y_sample)
```

```python
import functools
import math

import jax
import jax.numpy as jnp
from jax import lax
from jax.experimental import pallas as pl
from jax.experimental.pallas import tpu as pltpu

D_MODEL = 1024
N_META = 16
HEAD_DIM = 64
BLOCK = 128
WINDOW = 128
ROPE_THETA = 10000.0
EPS = 1e-6
NEG_INF = -1e30

A_WIDTH = 512
A_HEADS = 8
A_KV_HEADS = 2
B_WIDTH = 512
B_VDIM = 128
B_HEADS = 4
A_Q = A_HEADS * HEAD_DIM
A_KV = A_KV_HEADS * HEAD_DIM
B_QK = 2 * B_HEADS * HEAD_DIM
IN_WIDTH = A_Q + 2 * A_KV + A_WIDTH + 2 * B_QK + 2 * B_WIDTH

OFF_AQ = 0
OFF_AK = OFF_AQ + A_Q
OFF_AV = OFF_AK + A_KV
OFF_AG = OFF_AV + A_KV
OFF_BQ = OFF_AG + A_WIDTH
OFF_BK = OFF_BQ + B_QK
OFF_BV = OFF_BK + B_QK
OFF_BG = OFF_BV + B_WIDTH

LANES = 128
LOG2E = math.log2(math.e)
Q_SCALE = HEAD_DIM ** -0.5 * LOG2E

ROW_TILE = 512
DIFF_Q_BLOCK = 512
VMEM_LIMIT = 56 * 1024 * 1024

_NT = (((1,), (1,)), ((), ()))


def _cparams(sem):
    return pltpu.CompilerParams(dimension_semantics=sem, vmem_limit_bytes=VMEM_LIMIT)


def _in_proj_kernel(tblk_ref, h_ref, g_ref, w_ref, cos_ref, sin_ref,
                    aq_ref, ak_ref, aks_ref, av_ref, avt_ref, ag_ref,
                    bq_ref, bk_ref, bv_ref, bvt_ref, bg_ref):
    del tblk_ref
    x = h_ref[...]
    ms = jnp.mean(x * x, axis=-1, keepdims=True)
    hn = (x * lax.rsqrt(ms + EPS) * g_ref[...]).astype(jnp.bfloat16)
    cos = cos_ref[...]
    sin = sin_ref[...]
    lane = lax.broadcasted_iota(jnp.int32, (1, LANES), 1)
    first_half = (lane % HEAD_DIM) < (HEAD_DIM // 2)

    def proj(off, width):
        return jnp.dot(hn, w_ref[:, off:off + width], preferred_element_type=jnp.float32)

    def rope(p):
        fwd = pltpu.roll(p, LANES - HEAD_DIM // 2, 1)
        bwd = pltpu.roll(p, HEAD_DIM // 2, 1)
        return p * cos + jnp.where(first_half, fwd, bwd) * sin

    def silu(p):
        return p * (1.0 / (1.0 + jnp.exp(-p)))

    tm = x.shape[0]

    p = proj(OFF_AQ, A_Q)
    for j in range(A_Q // LANES):
        sl = slice(j * LANES, (j + 1) * LANES)
        aq_ref[:, sl] = (rope(p[:, sl]) * Q_SCALE).astype(jnp.bfloat16)

    p = proj(OFF_AK, 2 * A_KV)
    k = rope(p[:, :A_KV])
    ak_ref[...] = k.astype(jnp.bfloat16)
    aks_ref[...] = pltpu.roll(k, HEAD_DIM, 1).astype(jnp.bfloat16)
    v = p[:, A_KV:]
    av_ref[...] = v.astype(jnp.bfloat16)
    vt = v.T
    for j in range(tm // LANES):
        avt_ref[j] = vt[:, j * LANES:(j + 1) * LANES].astype(jnp.bfloat16)

    ag_ref[...] = silu(proj(OFF_AG, A_WIDTH)).astype(jnp.bfloat16)

    p = proj(OFF_BQ, B_QK)
    for j in range(B_QK // LANES):
        sl = slice(j * LANES, (j + 1) * LANES)
        bq_ref[:, sl] = (rope(p[:, sl]) * Q_SCALE).astype(jnp.bfloat16)

    p = proj(OFF_BK, B_QK)
    for j in range(B_QK // LANES):
        sl = slice(j * LANES, (j + 1) * LANES)
        bk_ref[:, sl] = rope(p[:, sl]).astype(jnp.bfloat16)

    p = proj(OFF_BV, B_WIDTH)
    bv_ref[...] = p.astype(jnp.bfloat16)
    bvt_ref[0] = p.T.astype(jnp.bfloat16)

    bg_ref[...] = silu(proj(OFF_BG, B_WIDTH)).astype(jnp.bfloat16)


def _in_proj(h, pre_g, w_bf16, cos_tab, sin_tab, tile_to_tab, tm):
    rows = h.shape[0]
    nt = rows // tm
    bf = jnp.bfloat16

    def row_spec(width):
        return pl.BlockSpec((tm, width), lambda t, tb: (t, 0))

    out_shape = (
        jax.ShapeDtypeStruct((rows, A_Q), bf),
        jax.ShapeDtypeStruct((rows, A_KV), bf),
        jax.ShapeDtypeStruct((rows, A_KV), bf),
        jax.ShapeDtypeStruct((rows, A_KV), bf),
        jax.ShapeDtypeStruct((rows // LANES, A_KV, LANES), bf),
        jax.ShapeDtypeStruct((rows, A_WIDTH), bf),
        jax.ShapeDtypeStruct((rows, B_QK), bf),
        jax.ShapeDtypeStruct((rows, B_QK), bf),
        jax.ShapeDtypeStruct((rows, B_WIDTH), bf),
        jax.ShapeDtypeStruct((nt, B_WIDTH, tm), bf),
        jax.ShapeDtypeStruct((rows, B_WIDTH), bf),
    )
    out_specs = (
        row_spec(A_Q), row_spec(A_KV), row_spec(A_KV), row_spec(A_KV),
        pl.BlockSpec((tm // LANES, A_KV, LANES), lambda t, tb: (t, 0, 0)),
        row_spec(A_WIDTH), row_spec(B_QK), row_spec(B_QK), row_spec(B_WIDTH),
        pl.BlockSpec((1, B_WIDTH, tm), lambda t, tb: (t, 0, 0)),
        row_spec(B_WIDTH),
    )
    grid_spec = pltpu.PrefetchScalarGridSpec(
        num_scalar_prefetch=1,
        grid=(nt,),
        in_specs=[
            row_spec(D_MODEL),
            pl.BlockSpec((1, D_MODEL), lambda t, tb: (0, 0)),
            pl.BlockSpec((D_MODEL, IN_WIDTH), lambda t, tb: (0, 0)),
            pl.BlockSpec((tm, LANES), lambda t, tb: (tb[t], 0)),
            pl.BlockSpec((tm, LANES), lambda t, tb: (tb[t], 0)),
        ],
        out_specs=out_specs,
    )
    return pl.pallas_call(
        _in_proj_kernel, grid_spec=grid_spec, out_shape=out_shape,
        compiler_params=_cparams(("parallel",)), name="in_proj",
    )(tile_to_tab, h, pre_g, w_bf16, cos_tab, sin_tab)


def _out_proj_kernel(oa_ref, ob_ref, wa_ref, wb_ref, g_ref, h_ref, out_ref):
    o = jnp.dot(oa_ref[...], wa_ref[...], preferred_element_type=jnp.float32)
    o = o + jnp.dot(ob_ref[...], wb_ref[...], preferred_element_type=jnp.float32)
    ms = jnp.mean(o * o, axis=-1, keepdims=True)
    out_ref[...] = h_ref[...] + o * lax.rsqrt(ms + EPS) * g_ref[...]


def _out_proj(oa, ob, wa, wb, post_g, h, tm):
    rows = h.shape[0]
    return pl.pallas_call(
        _out_proj_kernel,
        grid=(rows // tm,),
        in_specs=[
            pl.BlockSpec((tm, A_WIDTH), lambda t: (t, 0)),
            pl.BlockSpec((tm, B_WIDTH), lambda t: (t, 0)),
            pl.BlockSpec((A_WIDTH, D_MODEL), lambda t: (0, 0)),
            pl.BlockSpec((B_WIDTH, D_MODEL), lambda t: (0, 0)),
            pl.BlockSpec((1, D_MODEL), lambda t: (0, 0)),
            pl.BlockSpec((tm, D_MODEL), lambda t: (t, 0)),
        ],
        out_specs=pl.BlockSpec((tm, D_MODEL), lambda t: (t, 0)),
        out_shape=jax.ShapeDtypeStruct(h.shape, h.dtype),
        input_output_aliases={5: 0},
        compiler_params=_cparams(("parallel",)), name="out_proj",
    )(oa, ob, wa, wb, post_g, h)


def _pad_rows(x, rows):
    if x.shape[0] == rows:
        return x
    return jnp.concatenate([x, jnp.zeros((rows - x.shape[0], x.shape[1]), x.dtype)], axis=0)


def _meta_kv(km_ref, vm_ref):
    kmp = _pad_rows(km_ref[...], LANES)
    vmt = _pad_rows(vm_ref[...].astype(jnp.float32), LANES).T.astype(jnp.bfloat16)
    return kmp, vmt


def _diff_attn_kernel(lq1_ref, lk1_ref, lq2_ref, lk2_ref, sg_ref,
                      q_ref, k_ref, km_ref, vt_ref, vm_ref, gate_ref, o_ref,
                      qcat_sc, m_sc, l_sc, acc_sc,
                      *, n_chunks, ck, n_units, nq, lambda_init):
    f32 = jnp.float32
    lam = (jnp.exp(jnp.sum(lq1_ref[...] * lk1_ref[...], axis=-1, keepdims=True))
           - jnp.exp(jnp.sum(lq2_ref[...] * lk2_ref[...], axis=-1, keepdims=True)) + lambda_init)
    lane = lax.broadcasted_iota(jnp.int32, (1, LANES), 1)
    comp0 = lane < HEAD_DIM
    kmp, vmt = _meta_kv(km_ref, vm_ref)
    meta_valid = lax.broadcasted_iota(jnp.int32, (LANES, 1), 0) < N_META

    for u in range(n_units):
        q = _pad_rows(q_ref[u * nq:(u + 1) * nq, :], LANES)
        zero = jnp.zeros_like(q)
        qcat = jnp.concatenate([jnp.where(comp0, q, zero), jnp.where(comp0, zero, q)], axis=0)
        qcat_sc[u] = qcat
        s = lax.dot_general(kmp, qcat, _NT, preferred_element_type=f32)
        s = jnp.where(meta_valid, s, NEG_INF)
        m = jnp.max(s, axis=0, keepdims=True)
        p = jnp.exp2(s - m)
        m_sc[u] = m
        l_sc[u] = jnp.sum(p, axis=0, keepdims=True)
        acc_sc[u] = jnp.dot(vmt, p.astype(jnp.bfloat16), preferred_element_type=f32)

    def chunk(c, carry):
        kc = k_ref[pl.ds(pl.multiple_of(c * ck, ck), ck), :]
        vtc = vt_ref[c]
        for u in range(n_units):
            s = lax.dot_general(kc, qcat_sc[u], _NT, preferred_element_type=f32)
            m_old = m_sc[u]
            m_new = jnp.maximum(m_old, jnp.max(s, axis=0, keepdims=True))
            alpha = jnp.exp2(m_old - m_new)
            p = jnp.exp2(s - m_new)
            l_sc[u] = alpha * l_sc[u] + jnp.sum(p, axis=0, keepdims=True)
            acc_sc[u] = alpha * acc_sc[u] + jnp.dot(vtc, p.astype(jnp.bfloat16), preferred_element_type=f32)
            m_sc[u] = m_new
        return carry

    lax.fori_loop(0, n_chunks, chunk, 0)

    sg = sg_ref[...]
    for u in range(n_units):
        r = 1.0 / l_sc[u]
        a = acc_sc[u] * r
        o = a[:, :LANES] - lam * a[:, LANES:]
        ms = jnp.mean(o * o, axis=0, keepdims=True)
        y = (o * lax.rsqrt(ms + EPS) * sg) * (1.0 - lambda_init)
        yt = y.T[:nq, :]
        g = gate_ref[u * nq:(u + 1) * nq, :].astype(f32)
        o_ref[u * nq:(u + 1) * nq, :] = (yt * g).astype(o_ref.dtype)


def _diff_attn(lams, sg, bq, bk, bvt, bv, bgate, ob_prev, *, bsz, seq, q_row0, kv_row0, meta_row0,
               tm, q_block, meta_q, lambda_init):
    rows = bq.shape[0]
    n_chunks = seq // tm
    if meta_q:
        nq, n_units, nqb = N_META, 1, 1
        qblk = N_META
        q_blk0 = meta_row0 // N_META
        q_index = lambda b, h, i: (q_blk0 + b, h)
    else:
        nq, n_units, nqb = LANES, q_block // LANES, seq // q_block
        qblk = q_block
        q_blk0 = q_row0 // q_block
        q_index = lambda b, h, i: (q_blk0 + b * nqb + i, h)
    kv_blk0 = kv_row0 // seq
    meta_blk0 = meta_row0 // N_META
    vt_blk0 = (kv_row0 // tm) // n_chunks

    kernel = functools.partial(_diff_attn_kernel, n_chunks=n_chunks, ck=tm, n_units=n_units, nq=nq,
                               lambda_init=lambda_init)
    lam_spec = pl.BlockSpec((1, HEAD_DIM), lambda b, h, i: (0, 0))
    in_specs = [
        lam_spec, lam_spec, lam_spec, lam_spec,
        pl.BlockSpec((B_VDIM, 1), lambda b, h, i: (0, 0)),
        pl.BlockSpec((qblk, LANES), q_index),
        pl.BlockSpec((seq, LANES), lambda b, h, i: (kv_blk0 + b, h)),
        pl.BlockSpec((N_META, LANES), lambda b, h, i: (meta_blk0 + b, h)),
        pl.BlockSpec((n_chunks, B_VDIM, tm), lambda b, h, i: (vt_blk0 + b, h, 0)),
        pl.BlockSpec((N_META, B_VDIM), lambda b, h, i: (meta_blk0 + b, h)),
        pl.BlockSpec((qblk, B_VDIM), q_index),
    ]
    args = [*lams, sg, bq, bk, bk, bvt, bv, bgate]
    aliases = {}
    if ob_prev is not None:
        in_specs.append(pl.BlockSpec(memory_space=pl.ANY))
        args.append(ob_prev)
        aliases = {len(args) - 1: 0}
        kernel = functools.partial(_drop_arg, kernel, len(args) - 1)
    return pl.pallas_call(
        kernel,
        grid=(bsz, B_HEADS, nqb),
        in_specs=in_specs,
        out_specs=pl.BlockSpec((qblk, B_VDIM), q_index),
        out_shape=jax.ShapeDtypeStruct((rows, B_WIDTH), jnp.bfloat16),
        scratch_shapes=[
            pltpu.VMEM((n_units, 2 * LANES, LANES), jnp.bfloat16),
            pltpu.VMEM((n_units, 1, 2 * LANES), jnp.float32),
            pltpu.VMEM((n_units, 1, 2 * LANES), jnp.float32),
            pltpu.VMEM((n_units, B_VDIM, 2 * LANES), jnp.float32),
        ],
        input_output_aliases=aliases,
        compiler_params=_cparams(("parallel", "parallel", "arbitrary")),
        name="diff_attn_meta" if meta_q else "diff_attn",
    )(*args)


def _drop_arg(kernel, pos, *refs):
    return kernel(*refs[:pos], *refs[pos + 1:])


def _window_attn_kernel(sink_ref, q_ref, k_ref, ks_ref, km_ref, kms_ref, vt_ref, vm_ref, gate_ref, o_ref,
                        *, nb, nq, meta_q):
    f32 = jnp.float32
    bf = jnp.bfloat16
    n = pl.program_id(1)
    lane = lax.broadcasted_iota(jnp.int32, (1, LANES), 1)
    lo = lane < HEAD_DIM

    q = _pad_rows(q_ref[...], LANES)
    zero = jnp.zeros((LANES, LANES), bf)
    cols = [q[:, j * LANES:(j + 1) * LANES] for j in range(A_Q // LANES)]
    lo_of = [jnp.where(lo, c, zero) for c in cols]
    hi_of = [jnp.where(lo, zero, c) for c in cols]
    heads_plain = (0, 2, 5, 7)
    heads_swap = (1, 3, 4, 6)
    q_plain = jnp.concatenate([lo_of[0], lo_of[1], hi_of[2], hi_of[3]], axis=0)
    q_swap = jnp.concatenate([hi_of[0], hi_of[1], lo_of[2], lo_of[3]], axis=0)

    kmp, vmt = _meta_kv(km_ref, vm_ref)
    kmsp = _pad_rows(kms_ref[...], LANES)
    key_i = lax.broadcasted_iota(jnp.int32, (LANES, LANES), 0)
    qry_i = lax.broadcasted_iota(jnp.int32, (LANES, LANES), 1)
    meta_mask = key_i < N_META
    if meta_q:
        pieces_k = [kmp, k_ref[0:BLOCK, :]]
        pieces_ks = [kmsp, ks_ref[0:BLOCK, :]]
        pieces_vt = [vmt, vt_ref[0]]
        masks = [meta_mask, (N_META + key_i - qry_i) <= WINDOW]
    else:
        prev = jnp.maximum(n - 1, 0)
        nxt = jnp.minimum(n + 1, nb - 1)

        def rows_of(ref, blk):
            return ref[pl.ds(pl.multiple_of(blk * BLOCK, BLOCK), BLOCK), :]

        pieces_k = [kmp, rows_of(k_ref, prev), rows_of(k_ref, n), rows_of(k_ref, nxt)]
        pieces_ks = [kmsp, rows_of(ks_ref, prev), rows_of(ks_ref, n), rows_of(ks_ref, nxt)]
        pieces_vt = [vmt, vt_ref[prev], vt_ref[n], vt_ref[nxt]]
        masks = [meta_mask,
                 (key_i >= qry_i) & (n > 0),
                 jnp.full((LANES, LANES), True),
                 (key_i <= qry_i) & (n < nb - 1)]
    kcat = jnp.concatenate(pieces_k, axis=0)
    kscat = jnp.concatenate(pieces_ks, axis=0)
    vtcat = jnp.concatenate(pieces_vt, axis=1)
    mask = jnp.concatenate(masks, axis=0)

    out_heads = [None] * A_HEADS
    for kmat, qmat, heads in ((kcat, q_plain, heads_plain), (kscat, q_swap, heads_swap)):
        s = lax.dot_general(kmat, qmat, _NT, preferred_element_type=f32)
        ps = []
        rs = []
        for t, hd in enumerate(heads):
            sh = jnp.where(mask, s[:, t * LANES:(t + 1) * LANES], NEG_INF)
            sink = sink_ref[hd] * LOG2E
            m = jnp.maximum(jnp.max(sh, axis=0, keepdims=True), sink)
            p = jnp.exp2(sh - m)
            l = jnp.sum(p, axis=0, keepdims=True) + jnp.exp2(sink - m)
            ps.append(p.astype(bf))
            rs.append(1.0 / l)
        ot = jnp.dot(vtcat, jnp.concatenate(ps, axis=1), preferred_element_type=f32)
        for t, hd in enumerate(heads):
            g = hd // (A_HEADS // A_KV_HEADS)
            out_heads[hd] = ot[g * HEAD_DIM:(g + 1) * HEAD_DIM, t * LANES:(t + 1) * LANES] * rs[t]
    ot_all = jnp.concatenate(out_heads, axis=0)
    o = ot_all.T[:nq, :]
    o_ref[...] = (o * gate_ref[...].astype(f32)).astype(o_ref.dtype)


def _window_attn(sink, aq, ak, aks, avt, av, agate, oa_prev, *, bsz, seq, q_row0, kv_row0, meta_row0, meta_q):
    rows = aq.shape[0]
    nb = seq // BLOCK
    meta_blk0 = meta_row0 // N_META
    kv_blk0 = kv_row0 // seq
    vt_blk0 = (kv_row0 // BLOCK) // nb
    if meta_q:
        nq, nqb = N_META, 1
        q_index = lambda b, n: (meta_blk0 + b, 0)
        k_spec = pl.BlockSpec((BLOCK, A_KV), lambda b, n: ((kv_row0 // BLOCK) + b * nb, 0))
        vt_spec = pl.BlockSpec((1, A_KV, LANES), lambda b, n: ((kv_row0 // BLOCK) + b * nb, 0, 0))
    else:
        nq, nqb = BLOCK, nb
        q_blk0 = q_row0 // BLOCK
        q_index = lambda b, n: (q_blk0 + b * nb + n, 0)
        k_spec = pl.BlockSpec((seq, A_KV), lambda b, n: (kv_blk0 + b, 0))
        vt_spec = pl.BlockSpec((nb, A_KV, LANES), lambda b, n: (vt_blk0 + b, 0, 0))
    meta_spec = pl.BlockSpec((N_META, A_KV), lambda b, n: (meta_blk0 + b, 0))
    kernel = functools.partial(_window_attn_kernel, nb=nb, nq=nq, meta_q=meta_q)
    in_specs = [
        pl.BlockSpec(memory_space=pltpu.SMEM),
        pl.BlockSpec((nq, A_Q), q_index),
        k_spec, k_spec, meta_spec, meta_spec, vt_spec, meta_spec,
        pl.BlockSpec((nq, A_WIDTH), q_index),
    ]
    args = [sink, aq, ak, aks, ak, aks, avt, av, agate]
    aliases = {}
    if oa_prev is not None:
        in_specs.append(pl.BlockSpec(memory_space=pl.ANY))
        args.append(oa_prev)
        aliases = {len(args) - 1: 0}
        kernel = functools.partial(_drop_arg, kernel, len(args) - 1)
    return pl.pallas_call(
        kernel,
        grid=(bsz, nqb),
        in_specs=in_specs,
        out_specs=pl.BlockSpec((nq, A_WIDTH), q_index),
        out_shape=jax.ShapeDtypeStruct((rows, A_WIDTH), jnp.bfloat16),
        input_output_aliases=aliases,
        compiler_params=_cparams(("parallel", "arbitrary")),
        name="window_attn_meta" if meta_q else "window_attn",
    )(*args)


def _rope_tables(smax, tm):
    inv_freq = 1.0 / (ROPE_THETA ** (jnp.arange(0, HEAD_DIM, 2, dtype=jnp.float32) / HEAD_DIM))
    pos = jnp.concatenate([N_META + jnp.arange(smax, dtype=jnp.float32),
                           jnp.tile(jnp.arange(N_META, dtype=jnp.float32), tm // N_META)])
    ang = pos[:, None] * inv_freq[None, :]
    ang = jnp.concatenate([ang, ang, ang, ang], axis=-1)
    sign = jnp.where((jnp.arange(LANES) % HEAD_DIM) < HEAD_DIM // 2, -1.0, 1.0).astype(jnp.float32)
    return jnp.cos(ang), jnp.sin(ang) * sign[None, :]


def _encode_both(x_prompt, x_sample, meta_tokens, w_in, w_out, pre_norm_g, post_norm_g, sink_logits,
                 lambda_q1, lambda_k1, lambda_q2, lambda_k2, subln_g, *, tm, q_block):
    bp, sp, d = x_prompt.shape
    bs, ss, _ = x_sample.shape
    depth = w_in.shape[0]
    assert d == D_MODEL and sp % tm == 0 and ss % tm == 0 and sp % q_block == 0 and ss % q_block == 0
    assert (bp * sp) % ss == 0 and tm % LANES == 0 and q_block % LANES == 0

    n_real = bp * sp + bs * ss
    n_meta = (bp + bs) * N_META
    rows = -(-(n_real + n_meta) // tm) * tm
    groups = (
        dict(bsz=bp, seq=sp, q_row0=0, kv_row0=0, meta_row0=n_real),
        dict(bsz=bs, seq=ss, q_row0=bp * sp, kv_row0=bp * sp, meta_row0=n_real + bp * N_META),
    )
    meta_rows = jnp.broadcast_to(meta_tokens[None], (bp + bs, N_META, d)).reshape(n_meta, d)
    h = jnp.concatenate([x_prompt.reshape(bp * sp, d), x_sample.reshape(bs * ss, d), meta_rows,
                         jnp.zeros((rows - n_real - n_meta, d), x_prompt.dtype)], axis=0)

    smax = max(sp, ss)
    cos_tab, sin_tab = _rope_tables(smax, tm)
    tiles = jnp.arange(rows // tm, dtype=jnp.int32)
    ntp = bp * sp // tm
    nts = bs * ss // tm
    tile_to_tab = jnp.where(tiles < ntp, tiles % (sp // tm),
                            jnp.where(tiles < ntp + nts, (tiles - ntp) % (ss // tm), smax // tm)).astype(jnp.int32)

    for l in range(depth):
        lambda_init = 0.8 - 0.6 * math.exp(-0.3 * l)
        (aq, ak, aks, av, avt, agate, bq, bk, bv, bvt, bgate) = _in_proj(
            h, pre_norm_g[l][None, :], w_in[l].astype(jnp.bfloat16), cos_tab, sin_tab, tile_to_tab, tm)
        lams = (lambda_q1[l][None, :], lambda_k1[l][None, :], lambda_q2[l][None, :], lambda_k2[l][None, :])
        sg = subln_g[l][:, None]
        oa = None
        ob = None
        for meta_q in (False, True):
            for grp in groups:
                oa = _window_attn(sink_logits[l], aq, ak, aks, avt, av, agate, oa, meta_q=meta_q, **grp)
                ob = _diff_attn(lams, sg, bq, bk, bvt, bv, bgate, ob, tm=tm, q_block=q_block, meta_q=meta_q,
                                lambda_init=lambda_init, **grp)
        w_o = w_out[l].astype(jnp.bfloat16)
        h = _out_proj(oa, ob, w_o[:A_WIDTH], w_o[A_WIDTH:], post_norm_g[l][None, :], h, tm)

    y_prompt = h[:bp * sp].reshape(bp, sp, d)
    y_sample = h[bp * sp:n_real].reshape(bs, ss, d)
    return y_prompt, y_sample


def kernel(x_prompt, x_sample, meta_tokens, w_in, w_out, pre_norm_g, post_norm_g, sink_logits,
           lambda_q1, lambda_k1, lambda_q2, lambda_k2, subln_g):
    return _encode_both(x_prompt, x_sample, meta_tokens, w_in, w_out, pre_norm_g, post_norm_g, sink_logits,
                        lambda_q1, lambda_k1, lambda_q2, lambda_k2, subln_g,
                        tm=ROW_TILE, q_block=DIFF_Q_BLOCK)
```

```python
import functools
import math

import jax
import jax.numpy as jnp
from jax import lax
from jax.experimental import pallas as pl
from jax.experimental.pallas import tpu as pltpu

D_MODEL = 1024
N_META = 16
HEAD_DIM = 64
BLOCK = 128
WINDOW = 128
ROPE_THETA = 10000.0
EPS = 1e-6
NEG_INF = -1e30

A_WIDTH = 512
A_HEADS = 8
A_KV_HEADS = 2
B_WIDTH = 512
B_VDIM = 128
B_HEADS = 4
A_Q = A_HEADS * HEAD_DIM
A_KV = A_KV_HEADS * HEAD_DIM
B_QK = 2 * B_HEADS * HEAD_DIM
IN_WIDTH = A_Q + 2 * A_KV + A_WIDTH + 2 * B_QK + 2 * B_WIDTH

OFF_AQ = 0
OFF_AK = OFF_AQ + A_Q
OFF_AV = OFF_AK + A_KV
OFF_AG = OFF_AV + A_KV
OFF_BQ = OFF_AG + A_WIDTH
OFF_BK = OFF_BQ + B_QK
OFF_BV = OFF_BK + B_QK
OFF_BG = OFF_BV + B_WIDTH

LANES = 128
LOG2E = math.log2(math.e)
Q_SCALE = HEAD_DIM ** -0.5 * LOG2E

ROW_TILE = 512
DIFF_Q_BLOCK = 512
VMEM_LIMIT = 56 * 1024 * 1024

_NT = (((1,), (1,)), ((), ()))


def _cparams(sem):
    return pltpu.CompilerParams(dimension_semantics=sem, vmem_limit_bytes=VMEM_LIMIT)


def _in_proj_kernel(tblk_ref, h_ref, g_ref, w_ref, cos_ref, sin_ref,
                    aq_ref, ak_ref, aks_ref, av_ref, avt_ref, ag_ref,
                    bq_ref, bk_ref, bv_ref, bvt_ref, bg_ref):
    del tblk_ref
    x = h_ref[...]
    ms = jnp.mean(x * x, axis=-1, keepdims=True)
    hn = (x * lax.rsqrt(ms + EPS) * g_ref[...]).astype(jnp.bfloat16)
    cos = cos_ref[...]
    sin = sin_ref[...]
    lane = lax.broadcasted_iota(jnp.int32, (1, LANES), 1)
    first_half = (lane % HEAD_DIM) < (HEAD_DIM // 2)

    def proj(off, width):
        return jnp.dot(hn, w_ref[:, off:off + width], preferred_element_type=jnp.float32)

    def rope(p):
        fwd = pltpu.roll(p, LANES - HEAD_DIM // 2, 1)
        bwd = pltpu.roll(p, HEAD_DIM // 2, 1)
        return p * cos + jnp.where(first_half, fwd, bwd) * sin

    def silu(p):
        return p * (1.0 / (1.0 + jnp.exp(-p)))

    tm = x.shape[0]

    p = proj(OFF_AQ, A_Q)
    for j in range(A_Q // LANES):
        sl = slice(j * LANES, (j + 1) * LANES)
        aq_ref[:, sl] = (rope(p[:, sl]) * Q_SCALE).astype(jnp.bfloat16)

    p = proj(OFF_AK, 2 * A_KV)
    k = rope(p[:, :A_KV])
    ak_ref[...] = k.astype(jnp.bfloat16)
    aks_ref[...] = pltpu.roll(k, HEAD_DIM, 1).astype(jnp.bfloat16)
    v = p[:, A_KV:]
    av_ref[...] = v.astype(jnp.bfloat16)
    vt = v.T
    for j in range(tm // LANES):
        avt_ref[j] = vt[:, j * LANES:(j + 1) * LANES].astype(jnp.bfloat16)

    ag_ref[...] = silu(proj(OFF_AG, A_WIDTH)).astype(jnp.bfloat16)

    p = proj(OFF_BQ, B_QK)
    for j in range(B_QK // LANES):
        sl = slice(j * LANES, (j + 1) * LANES)
        bq_ref[:, sl] = (rope(p[:, sl]) * Q_SCALE).astype(jnp.bfloat16)

    p = proj(OFF_BK, B_QK)
    for j in range(B_QK // LANES):
        sl = slice(j * LANES, (j + 1) * LANES)
        bk_ref[:, sl] = rope(p[:, sl]).astype(jnp.bfloat16)

    p = proj(OFF_BV, B_WIDTH)
    bv_ref[...] = p.astype(jnp.bfloat16)
    bvt_ref[0] = p.T.astype(jnp.bfloat16)

    bg_ref[...] = silu(proj(OFF_BG, B_WIDTH)).astype(jnp.bfloat16)


def _in_proj(h, pre_g, w_bf16, cos_tab, sin_tab, tile_to_tab, tm):
    rows = h.shape[0]
    nt = rows // tm
    bf = jnp.bfloat16

    def row_spec(width):
        return pl.BlockSpec((tm, width), lambda t, tb: (t, 0))

    out_shape = (
        jax.ShapeDtypeStruct((rows, A_Q), bf),
        jax.ShapeDtypeStruct((rows, A_KV), bf),
        jax.ShapeDtypeStruct((rows, A_KV), bf),
        jax.ShapeDtypeStruct((rows, A_KV), bf),
        jax.ShapeDtypeStruct((rows // LANES, A_KV, LANES), bf),
        jax.ShapeDtypeStruct((rows, A_WIDTH), bf),
        jax.ShapeDtypeStruct((rows, B_QK), bf),
        jax.ShapeDtypeStruct((rows, B_QK), bf),
        jax.ShapeDtypeStruct((rows, B_WIDTH), bf),
        jax.ShapeDtypeStruct((nt, B_WIDTH, tm), bf),
        jax.ShapeDtypeStruct((rows, B_WIDTH), bf),
    )
    out_specs = (
        row_spec(A_Q), row_spec(A_KV), row_spec(A_KV), row_spec(A_KV),
        pl.BlockSpec((tm // LANES, A_KV, LANES), lambda t, tb: (t, 0, 0)),
        row_spec(A_WIDTH), row_spec(B_QK), row_spec(B_QK), row_spec(B_WIDTH),
        pl.BlockSpec((1, B_WIDTH, tm), lambda t, tb: (t, 0, 0)),
        row_spec(B_WIDTH),
    )
    grid_spec = pltpu.PrefetchScalarGridSpec(
        num_scalar_prefetch=1,
        grid=(nt,),
        in_specs=[
            row_spec(D_MODEL),
            pl.BlockSpec((1, D_MODEL), lambda t, tb: (0, 0)),
            pl.BlockSpec((D_MODEL, IN_WIDTH), lambda t, tb: (0, 0)),
            pl.BlockSpec((tm, LANES), lambda t, tb: (tb[t], 0)),
            pl.BlockSpec((tm, LANES), lambda t, tb: (tb[t], 0)),
        ],
        out_specs=out_specs,
    )
    return pl.pallas_call(
        _in_proj_kernel, grid_spec=grid_spec, out_shape=out_shape,
        compiler_params=_cparams(("parallel",)), name="in_proj",
    )(tile_to_tab, h, pre_g, w_bf16, cos_tab, sin_tab)


def _out_proj_kernel(oa_ref, ob_ref, wa_ref, wb_ref, g_ref, h_ref, out_ref):
    o = jnp.dot(oa_ref[...], wa_ref[...], preferred_element_type=jnp.float32)
    o = o + jnp.dot(ob_ref[...], wb_ref[...], preferred_element_type=jnp.float32)
    ms = jnp.mean(o * o, axis=-1, keepdims=True)
    out_ref[...] = h_ref[...] + o * lax.rsqrt(ms + EPS) * g_ref[...]


def _out_proj(oa, ob, wa, wb, post_g, h, tm):
    rows = h.shape[0]
    return pl.pallas_call(
        _out_proj_kernel,
        grid=(rows // tm,),
        in_specs=[
            pl.BlockSpec((tm, A_WIDTH), lambda t: (t, 0)),
            pl.BlockSpec((tm, B_WIDTH), lambda t: (t, 0)),
            pl.BlockSpec((A_WIDTH, D_MODEL), lambda t: (0, 0)),
            pl.BlockSpec((B_WIDTH, D_MODEL), lambda t: (0, 0)),
            pl.BlockSpec((1, D_MODEL), lambda t: (0, 0)),
            pl.BlockSpec((tm, D_MODEL), lambda t: (t, 0)),
        ],
        out_specs=pl.BlockSpec((tm, D_MODEL), lambda t: (t, 0)),
        out_shape=jax.ShapeDtypeStruct(h.shape, h.dtype),
        input_output_aliases={5: 0},
        compiler_params=_cparams(("parallel",)), name="out_proj",
    )(oa, ob, wa, wb, post_g, h)


def _pad_rows(x, rows):
    if x.shape[0] == rows:
        return x
    return jnp.concatenate([x, jnp.zeros((rows - x.shape[0], x.shape[1]), x.dtype)], axis=0)


def _meta_kv(km_ref, vm_ref):
    kmp = _pad_rows(km_ref[...], LANES)
    vmt = _pad_rows(vm_ref[...].astype(jnp.float32), LANES).T.astype(jnp.bfloat16)
    return kmp, vmt


def _diff_attn_kernel(lq1_ref, lk1_ref, lq2_ref, lk2_ref, sg_ref,
                      q_ref, k_ref, km_ref, vt_ref, vm_ref, gate_ref, o_ref,
                      qcat_sc, m_sc, l_sc, acc_sc, s_sc, mx_sc,
                      *, n_chunks, ck, n_units, nq, lambda_init):
    f32 = jnp.float32
    lam = (jnp.exp(jnp.sum(lq1_ref[...] * lk1_ref[...], axis=-1, keepdims=True))
           - jnp.exp(jnp.sum(lq2_ref[...] * lk2_ref[...], axis=-1, keepdims=True)) + lambda_init)
    lane = lax.broadcasted_iota(jnp.int32, (1, LANES), 1)
    comp0 = lane < HEAD_DIM
    kmp, vmt = _meta_kv(km_ref, vm_ref)
    meta_valid = lax.broadcasted_iota(jnp.int32, (LANES, 1), 0) < N_META

    qcats = []
    for u in range(n_units):
        q = _pad_rows(q_ref[u * nq:(u + 1) * nq, :], LANES)
        zero = jnp.zeros_like(q)
        qcats += [jnp.where(comp0, q, zero), jnp.where(comp0, zero, q)]
    qcat = jnp.concatenate(qcats, axis=0)
    qcat_sc[...] = qcat
    s = lax.dot_general(kmp, qcat, _NT, preferred_element_type=f32)
    s = jnp.where(meta_valid, s, NEG_INF)
    m = jnp.max(s, axis=0, keepdims=True)
    p = jnp.exp2(s - m)
    m_sc[...] = m
    l_sc[...] = jnp.sum(p, axis=0, keepdims=True)
    acc_sc[...] = jnp.dot(vmt, p.astype(jnp.bfloat16), preferred_element_type=f32)

    def scores(c, slot):
        kc = k_ref[pl.ds(pl.multiple_of(c * ck, ck), ck), :]
        s = lax.dot_general(kc, qcat_sc[...], _NT, preferred_element_type=f32)
        s_sc[slot] = s
        mx_sc[slot] = jnp.max(s, axis=0, keepdims=True)

    def softmax_pv(c, slot):
        m_old = m_sc[...]
        m_new = jnp.maximum(m_old, mx_sc[slot])
        alpha = jnp.exp2(m_old - m_new)
        p = jnp.exp2(s_sc[slot] - m_new)
        l_sc[...] = alpha * l_sc[...] + jnp.sum(p, axis=0, keepdims=True)
        acc_sc[...] = alpha * acc_sc[...] + jnp.dot(vt_ref[c], p.astype(jnp.bfloat16),
                                                    preferred_element_type=f32)
        m_sc[...] = m_new

    def pair(i, carry):
        c0 = 2 * i
        scores(c0 + 1, 1)
        softmax_pv(c0, 0)
        scores(c0 + 2, 0)
        softmax_pv(c0 + 1, 1)
        return carry

    assert n_chunks % 2 == 0
    scores(0, 0)
    lax.fori_loop(0, n_chunks // 2 - 1, pair, 0)
    scores(n_chunks - 1, 1)
    softmax_pv(n_chunks - 2, 0)
    softmax_pv(n_chunks - 1, 1)

    sg = sg_ref[...]
    a_all = acc_sc[...] * (1.0 / l_sc[...])
    for u in range(n_units):
        a = a_all[:, u * 2 * LANES:(u + 1) * 2 * LANES]
        o = a[:, :LANES] - lam * a[:, LANES:]
        ms = jnp.mean(o * o, axis=0, keepdims=True)
        y = (o * lax.rsqrt(ms + EPS) * sg) * (1.0 - lambda_init)
        yt = y.T[:nq, :]
        g = gate_ref[u * nq:(u + 1) * nq, :].astype(f32)
        o_ref[u * nq:(u + 1) * nq, :] = (yt * g).astype(o_ref.dtype)


def _diff_attn(lams, sg, bq, bk, bvt, bv, bgate, ob_prev, *, bsz, seq, q_row0, kv_row0, meta_row0,
               tm, q_block, meta_q, lambda_init):
    rows = bq.shape[0]
    n_chunks = seq // tm
    if meta_q:
        nq, n_units, nqb = N_META, 1, 1
        qblk = N_META
        q_blk0 = meta_row0 // N_META
        q_index = lambda b, h, i: (q_blk0 + b, h)
    else:
        nq, n_units, nqb = LANES, q_block // LANES, seq // q_block
        qblk = q_block
        q_blk0 = q_row0 // q_block
        q_index = lambda b, h, i: (q_blk0 + b * nqb + i, h)
    kv_blk0 = kv_row0 // seq
    meta_blk0 = meta_row0 // N_META
    vt_blk0 = (kv_row0 // tm) // n_chunks

    kernel = functools.partial(_diff_attn_kernel, n_chunks=n_chunks, ck=tm, n_units=n_units, nq=nq,
                               lambda_init=lambda_init)
    lam_spec = pl.BlockSpec((1, HEAD_DIM), lambda b, h, i: (0, 0))
    in_specs = [
        lam_spec, lam_spec, lam_spec, lam_spec,
        pl.BlockSpec((B_VDIM, 1), lambda b, h, i: (0, 0)),
        pl.BlockSpec((qblk, LANES), q_index),
        pl.BlockSpec((seq, LANES), lambda b, h, i: (kv_blk0 + b, h)),
        pl.BlockSpec((N_META, LANES), lambda b, h, i: (meta_blk0 + b, h)),
        pl.BlockSpec((n_chunks, B_VDIM, tm), lambda b, h, i: (vt_blk0 + b, h, 0)),
        pl.BlockSpec((N_META, B_VDIM), lambda b, h, i: (meta_blk0 + b, h)),
        pl.BlockSpec((qblk, B_VDIM), q_index),
    ]
    args = [*lams, sg, bq, bk, bk, bvt, bv, bgate]
    aliases = {}
    if ob_prev is not None:
        in_specs.append(pl.BlockSpec(memory_space=pl.ANY))
        args.append(ob_prev)
        aliases = {len(args) - 1: 0}
        kernel = functools.partial(_drop_arg, kernel, len(args) - 1)
    return pl.pallas_call(
        kernel,
        grid=(bsz, B_HEADS, nqb),
        in_specs=in_specs,
        out_specs=pl.BlockSpec((qblk, B_VDIM), q_index),
        out_shape=jax.ShapeDtypeStruct((rows, B_WIDTH), jnp.bfloat16),
        scratch_shapes=[
            pltpu.VMEM((n_units * 2 * LANES, LANES), jnp.bfloat16),
            pltpu.VMEM((1, n_units * 2 * LANES), jnp.float32),
            pltpu.VMEM((1, n_units * 2 * LANES), jnp.float32),
            pltpu.VMEM((B_VDIM, n_units * 2 * LANES), jnp.float32),
            pltpu.VMEM((2, tm, n_units * 2 * LANES), jnp.float32),
            pltpu.VMEM((2, 1, n_units * 2 * LANES), jnp.float32),
        ],
        input_output_aliases=aliases,
        compiler_params=_cparams(("parallel", "parallel", "arbitrary")),
        name="diff_attn_meta" if meta_q else "diff_attn",
    )(*args)


def _drop_arg(kernel, pos, *refs):
    return kernel(*refs[:pos], *refs[pos + 1:])


def _window_attn_kernel(sink_ref, q_ref, k_ref, ks_ref, km_ref, kms_ref, vt_ref, vm_ref, gate_ref, o_ref,
                        *, nb, nq, meta_q):
    f32 = jnp.float32
    bf = jnp.bfloat16
    n = pl.program_id(1)
    lane = lax.broadcasted_iota(jnp.int32, (1, LANES), 1)
    lo = lane < HEAD_DIM

    q = _pad_rows(q_ref[...], LANES)
    zero = jnp.zeros((LANES, LANES), bf)
    cols = [q[:, j * LANES:(j + 1) * LANES] for j in range(A_Q // LANES)]
    lo_of = [jnp.where(lo, c, zero) for c in cols]
    hi_of = [jnp.where(lo, zero, c) for c in cols]
    heads_plain = (0, 2, 5, 7)
    heads_swap = (1, 3, 4, 6)
    q_plain = jnp.concatenate([lo_of[0], lo_of[1], hi_of[2], hi_of[3]], axis=0)
    q_swap = jnp.concatenate([hi_of[0], hi_of[1], lo_of[2], lo_of[3]], axis=0)

    kmp, vmt = _meta_kv(km_ref, vm_ref)
    kmsp = _pad_rows(kms_ref[...], LANES)
    key_i = lax.broadcasted_iota(jnp.int32, (LANES, LANES), 0)
    qry_i = lax.broadcasted_iota(jnp.int32, (LANES, LANES), 1)
    meta_mask = key_i < N_META
    if meta_q:
        pieces_k = [kmp, k_ref[0:BLOCK, :]]
        pieces_ks = [kmsp, ks_ref[0:BLOCK, :]]
        pieces_vt = [vmt, vt_ref[0]]
        masks = [meta_mask, (N_META + key_i - qry_i) <= WINDOW]
    else:
        prev = jnp.maximum(n - 1, 0)
        nxt = jnp.minimum(n + 1, nb - 1)

        def rows_of(ref, blk):
            return ref[pl.ds(pl.multiple_of(blk * BLOCK, BLOCK), BLOCK), :]

        pieces_k = [kmp, rows_of(k_ref, prev), rows_of(k_ref, n), rows_of(k_ref, nxt)]
        pieces_ks = [kmsp, rows_of(ks_ref, prev), rows_of(ks_ref, n), rows_of(ks_ref, nxt)]
        pieces_vt = [vmt, vt_ref[prev], vt_ref[n], vt_ref[nxt]]
        masks = [meta_mask,
                 (key_i >= qry_i) & (n > 0),
                 jnp.full((LANES, LANES), True),
                 (key_i <= qry_i) & (n < nb - 1)]
    kcat = jnp.concatenate(pieces_k, axis=0)
    kscat = jnp.concatenate(pieces_ks, axis=0)
    vtcat = jnp.concatenate(pieces_vt, axis=1)
    mask = jnp.concatenate(masks, axis=0)

    out_heads = [None] * A_HEADS
    for kmat, qmat, heads in ((kcat, q_plain, heads_plain), (kscat, q_swap, heads_swap)):
        s = lax.dot_general(kmat, qmat, _NT, preferred_element_type=f32)
        ps = []
        rs = []
        for t, hd in enumerate(heads):
            sh = jnp.where(mask, s[:, t * LANES:(t + 1) * LANES], NEG_INF)
            sink = sink_ref[hd] * LOG2E
            m = jnp.maximum(jnp.max(sh, axis=0, keepdims=True), sink)
            p = jnp.exp2(sh - m)
            l = jnp.sum(p, axis=0, keepdims=True) + jnp.exp2(sink - m)
            ps.append(p.astype(bf))
            rs.append(1.0 / l)
        ot = jnp.dot(vtcat, jnp.concatenate(ps, axis=1), preferred_element_type=f32)
        for t, hd in enumerate(heads):
            g = hd // (A_HEADS // A_KV_HEADS)
            out_heads[hd] = ot[g * HEAD_DIM:(g + 1) * HEAD_DIM, t * LANES:(t + 1) * LANES] * rs[t]
    ot_all = jnp.concatenate(out_heads, axis=0)
    o = ot_all.T[:nq, :]
    o_ref[...] = (o * gate_ref[...].astype(f32)).astype(o_ref.dtype)


def _window_attn(sink, aq, ak, aks, avt, av, agate, oa_prev, *, bsz, seq, q_row0, kv_row0, meta_row0, meta_q):
    rows = aq.shape[0]
    nb = seq // BLOCK
    meta_blk0 = meta_row0 // N_META
    kv_blk0 = kv_row0 // seq
    vt_blk0 = (kv_row0 // BLOCK) // nb
    if meta_q:
        nq, nqb = N_META, 1
        q_index = lambda b, n: (meta_blk0 + b, 0)
        k_spec = pl.BlockSpec((BLOCK, A_KV), lambda b, n: ((kv_row0 // BLOCK) + b * nb, 0))
        vt_spec = pl.BlockSpec((1, A_KV, LANES), lambda b, n: ((kv_row0 // BLOCK) + b * nb, 0, 0))
    else:
        nq, nqb = BLOCK, nb
        q_blk0 = q_row0 // BLOCK
        q_index = lambda b, n: (q_blk0 + b * nb + n, 0)
        k_spec = pl.BlockSpec((seq, A_KV), lambda b, n: (kv_blk0 + b, 0))
        vt_spec = pl.BlockSpec((nb, A_KV, LANES), lambda b, n: (vt_blk0 + b, 0, 0))
    meta_spec = pl.BlockSpec((N_META, A_KV), lambda b, n: (meta_blk0 + b, 0))
    kernel = functools.partial(_window_attn_kernel, nb=nb, nq=nq, meta_q=meta_q)
    in_specs = [
        pl.BlockSpec(memory_space=pltpu.SMEM),
        pl.BlockSpec((nq, A_Q), q_index),
        k_spec, k_spec, meta_spec, meta_spec, vt_spec, meta_spec,
        pl.BlockSpec((nq, A_WIDTH), q_index),
    ]
    args = [sink, aq, ak, aks, ak, aks, avt, av, agate]
    aliases = {}
    if oa_prev is not None:
        in_specs.append(pl.BlockSpec(memory_space=pl.ANY))
        args.append(oa_prev)
        aliases = {len(args) - 1: 0}
        kernel = functools.partial(_drop_arg, kernel, len(args) - 1)
    return pl.pallas_call(
        kernel,
        grid=(bsz, nqb),
        in_specs=in_specs,
        out_specs=pl.BlockSpec((nq, A_WIDTH), q_index),
        out_shape=jax.ShapeDtypeStruct((rows, A_WIDTH), jnp.bfloat16),
        input_output_aliases=aliases,
        compiler_params=_cparams(("parallel", "arbitrary")),
        name="window_attn_meta" if meta_q else "window_attn",
    )(*args)


def _rope_tables(smax, tm):
    inv_freq = 1.0 / (ROPE_THETA ** (jnp.arange(0, HEAD_DIM, 2, dtype=jnp.float32) / HEAD_DIM))
    pos = jnp.concatenate([N_META + jnp.arange(smax, dtype=jnp.float32),
                           jnp.tile(jnp.arange(N_META, dtype=jnp.float32), tm // N_META)])
    ang = pos[:, None] * inv_freq[None, :]
    ang = jnp.concatenate([ang, ang, ang, ang], axis=-1)
    sign = jnp.where((jnp.arange(LANES) % HEAD_DIM) < HEAD_DIM // 2, -1.0, 1.0).astype(jnp.float32)
    return jnp.cos(ang), jnp.sin(ang) * sign[None, :]


def _encode_both(x_prompt, x_sample, meta_tokens, w_in, w_out, pre_norm_g, post_norm_g, sink_logits,
                 lambda_q1, lambda_k1, lambda_q2, lambda_k2, subln_g, *, tm, q_block):
    bp, sp, d = x_prompt.shape
    bs, ss, _ = x_sample.shape
    depth = w_in.shape[0]
    assert d == D_MODEL and sp % tm == 0 and ss % tm == 0 and sp % q_block == 0 and ss % q_block == 0
    assert (bp * sp) % ss == 0 and tm % LANES == 0 and q_block % LANES == 0

    n_real = bp * sp + bs * ss
    n_meta = (bp + bs) * N_META
    rows = -(-(n_real + n_meta) // tm) * tm
    groups = (
        dict(bsz=bp, seq=sp, q_row0=0, kv_row0=0, meta_row0=n_real),
        dict(bsz=bs, seq=ss, q_row0=bp * sp, kv_row0=bp * sp, meta_row0=n_real + bp * N_META),
    )
    meta_rows = jnp.broadcast_to(meta_tokens[None], (bp + bs, N_META, d)).reshape(n_meta, d)
    h = jnp.concatenate([x_prompt.reshape(bp * sp, d), x_sample.reshape(bs * ss, d), meta_rows,
                         jnp.zeros((rows - n_real - n_meta, d), x_prompt.dtype)], axis=0)

    smax = max(sp, ss)
    cos_tab, sin_tab = _rope_tables(smax, tm)
    tiles = jnp.arange(rows // tm, dtype=jnp.int32)
    ntp = bp * sp // tm
    nts = bs * ss // tm
    tile_to_tab = jnp.where(tiles < ntp, tiles % (sp // tm),
                            jnp.where(tiles < ntp + nts, (tiles - ntp) % (ss // tm), smax // tm)).astype(jnp.int32)

    for l in range(depth):
        lambda_init = 0.8 - 0.6 * math.exp(-0.3 * l)
        (aq, ak, aks, av, avt, agate, bq, bk, bv, bvt, bgate) = _in_proj(
            h, pre_norm_g[l][None, :], w_in[l].astype(jnp.bfloat16), cos_tab, sin_tab, tile_to_tab, tm)
        lams = (lambda_q1[l][None, :], lambda_k1[l][None, :], lambda_q2[l][None, :], lambda_k2[l][None, :])
        sg = subln_g[l][:, None]
        oa = None
        ob = None
        for meta_q in (False, True):
            for grp in groups:
                oa = _window_attn(sink_logits[l], aq, ak, aks, avt, av, agate, oa, meta_q=meta_q, **grp)
                ob = _diff_attn(lams, sg, bq, bk, bvt, bv, bgate, ob, tm=tm, q_block=q_block, meta_q=meta_q,
                                lambda_init=lambda_init, **grp)
        w_o = w_out[l].astype(jnp.bfloat16)
        h = _out_proj(oa, ob, w_o[:A_WIDTH], w_o[A_WIDTH:], post_norm_g[l][None, :], h, tm)

    y_prompt = h[:bp * sp].reshape(bp, sp, d)
    y_sample = h[bp * sp:n_real].reshape(bs, ss, d)
    return y_prompt, y_sample


def kernel(x_prompt, x_sample, meta_tokens, w_in, w_out, pre_norm_g, post_norm_g, sink_logits,
           lambda_q1, lambda_k1, lambda_q2, lambda_k2, subln_g):
    return _encode_both(x_prompt, x_sample, meta_tokens, w_in, w_out, pre_norm_g, post_norm_g, sink_logits,
                        lambda_q1, lambda_k1, lambda_q2, lambda_k2, subln_g,
                        tm=ROW_TILE, q_block=DIFF_Q_BLOCK)
```

```python
import functools
import math

import jax
import jax.numpy as jnp
from jax import lax
from jax.experimental import pallas as pl
from jax.experimental.pallas import tpu as pltpu

D_MODEL = 1024
N_META = 16
HEAD_DIM = 64
BLOCK = 128
WINDOW = 128
ROPE_THETA = 10000.0
EPS = 1e-6
NEG_INF = -1e30

A_WIDTH = 512
A_HEADS = 8
A_KV_HEADS = 2
B_WIDTH = 512
B_VDIM = 128
B_HEADS = 4
A_Q = A_HEADS * HEAD_DIM
A_KV = A_KV_HEADS * HEAD_DIM
B_QK = 2 * B_HEADS * HEAD_DIM
IN_WIDTH = A_Q + 2 * A_KV + A_WIDTH + 2 * B_QK + 2 * B_WIDTH

OFF_AQ = 0
OFF_AK = OFF_AQ + A_Q
OFF_AV = OFF_AK + A_KV
OFF_AG = OFF_AV + A_KV
OFF_BQ = OFF_AG + A_WIDTH
OFF_BK = OFF_BQ + B_QK
OFF_BV = OFF_BK + B_QK
OFF_BG = OFF_BV + B_WIDTH

LANES = 128
LOG2E = math.log2(math.e)
Q_SCALE = HEAD_DIM ** -0.5 * LOG2E

ROW_TILE = 512
DIFF_Q_BLOCK = 512
WINDOW_BLOCKS_PER_STEP = 8
VMEM_LIMIT = 56 * 1024 * 1024

_NT = (((1,), (1,)), ((), ()))


def _cparams(sem):
    return pltpu.CompilerParams(dimension_semantics=sem, vmem_limit_bytes=VMEM_LIMIT)


def _in_proj_kernel(h_ref, g_ref, w_ref, cos_ref, sin_ref,
                    aq_ref, ak_ref, aks_ref, av_ref, ag_ref, bq_ref, bk_ref, bv_ref, bg_ref,
                    *, transposed_v):
    x = h_ref[...]
    ms = jnp.mean(x * x, axis=-1, keepdims=True)
    hn = (x * lax.rsqrt(ms + EPS) * g_ref[...]).astype(jnp.bfloat16)
    cos = cos_ref[...]
    sin = sin_ref[...]
    lane = lax.broadcasted_iota(jnp.int32, (1, LANES), 1)
    first_half = (lane % HEAD_DIM) < (HEAD_DIM // 2)

    def proj(off, width):
        return jnp.dot(hn, w_ref[:, off:off + width], preferred_element_type=jnp.float32)

    def rope(p):
        fwd = pltpu.roll(p, LANES - HEAD_DIM // 2, 1)
        bwd = pltpu.roll(p, HEAD_DIM // 2, 1)
        return p * cos + jnp.where(first_half, fwd, bwd) * sin

    def silu(p):
        return p * (1.0 / (1.0 + jnp.exp(-p)))

    tm = x.shape[0]

    p = proj(OFF_AQ, A_Q)
    for j in range(A_Q // LANES):
        sl = slice(j * LANES, (j + 1) * LANES)
        aq_ref[:, sl] = (rope(p[:, sl]) * Q_SCALE).astype(jnp.bfloat16)

    p = proj(OFF_AK, 2 * A_KV)
    k = rope(p[:, :A_KV])
    ak_ref[...] = k.astype(jnp.bfloat16)
    aks_ref[...] = pltpu.roll(k, HEAD_DIM, 1).astype(jnp.bfloat16)
    v = p[:, A_KV:]
    if transposed_v:
        vt = v.T
        for j in range(tm // LANES):
            av_ref[j] = vt[:, j * LANES:(j + 1) * LANES].astype(jnp.bfloat16)
    else:
        av_ref[...] = v.astype(jnp.bfloat16)

    ag_ref[...] = silu(proj(OFF_AG, A_WIDTH)).astype(jnp.bfloat16)

    p = proj(OFF_BQ, B_QK)
    for j in range(B_QK // LANES):
        sl = slice(j * LANES, (j + 1) * LANES)
        bq_ref[:, sl] = (rope(p[:, sl]) * Q_SCALE).astype(jnp.bfloat16)

    p = proj(OFF_BK, B_QK)
    for j in range(B_QK // LANES):
        sl = slice(j * LANES, (j + 1) * LANES)
        bk_ref[:, sl] = rope(p[:, sl]).astype(jnp.bfloat16)

    p = proj(OFF_BV, B_WIDTH)
    if transposed_v:
        bv_ref[0] = p.T.astype(jnp.bfloat16)
    else:
        bv_ref[...] = p.astype(jnp.bfloat16)

    bg_ref[...] = silu(proj(OFF_BG, B_WIDTH)).astype(jnp.bfloat16)


def _in_proj(h, pre_g, w_bf16, cos_tab, sin_tab, tm, transposed_v):
    rows = h.shape[0]
    nt = rows // tm
    n_tab = cos_tab.shape[0] // tm
    bf = jnp.bfloat16

    def row_spec(width):
        return pl.BlockSpec((tm, width), lambda t: (t, 0))

    def row_shape(width):
        return jax.ShapeDtypeStruct((rows, width), bf)

    if transposed_v:
        av_shape = jax.ShapeDtypeStruct((rows // LANES, A_KV, LANES), bf)
        av_spec = pl.BlockSpec((tm // LANES, A_KV, LANES), lambda t: (t, 0, 0))
        bv_shape = jax.ShapeDtypeStruct((nt, B_WIDTH, tm), bf)
        bv_spec = pl.BlockSpec((1, B_WIDTH, tm), lambda t: (t, 0, 0))
    else:
        av_shape, av_spec = row_shape(A_KV), row_spec(A_KV)
        bv_shape, bv_spec = row_shape(B_WIDTH), row_spec(B_WIDTH)
    out_shape = (row_shape(A_Q), row_shape(A_KV), row_shape(A_KV), av_shape, row_shape(A_WIDTH),
                 row_shape(B_QK), row_shape(B_QK), bv_shape, row_shape(B_WIDTH))
    out_specs = (row_spec(A_Q), row_spec(A_KV), row_spec(A_KV), av_spec, row_spec(A_WIDTH),
                 row_spec(B_QK), row_spec(B_QK), bv_spec, row_spec(B_WIDTH))
    tab_spec = pl.BlockSpec((tm, LANES), lambda t: (t % n_tab, 0))
    return pl.pallas_call(
        functools.partial(_in_proj_kernel, transposed_v=transposed_v),
        grid=(nt,),
        in_specs=[
            row_spec(D_MODEL),
            pl.BlockSpec((1, D_MODEL), lambda t: (0, 0)),
            pl.BlockSpec((D_MODEL, IN_WIDTH), lambda t: (0, 0)),
            tab_spec, tab_spec,
        ],
        out_specs=out_specs, out_shape=out_shape,
        compiler_params=_cparams(("parallel",)), name="in_proj" if transposed_v else "in_proj_meta",
    )(h, pre_g, w_bf16, cos_tab, sin_tab)


def _out_proj_kernel(oa_ref, ob_ref, wa_ref, wb_ref, g_ref, h_ref, out_ref):
    o = jnp.dot(oa_ref[...], wa_ref[...], preferred_element_type=jnp.float32)
    o = o + jnp.dot(ob_ref[...], wb_ref[...], preferred_element_type=jnp.float32)
    ms = jnp.mean(o * o, axis=-1, keepdims=True)
    out_ref[...] = h_ref[...] + o * lax.rsqrt(ms + EPS) * g_ref[...]


def _out_proj(oa, ob, wa, wb, post_g, h, tm):
    rows = h.shape[0]
    return pl.pallas_call(
        _out_proj_kernel,
        grid=(rows // tm,),
        in_specs=[
            pl.BlockSpec((tm, A_WIDTH), lambda t: (t, 0)),
            pl.BlockSpec((tm, B_WIDTH), lambda t: (t, 0)),
            pl.BlockSpec((A_WIDTH, D_MODEL), lambda t: (0, 0)),
            pl.BlockSpec((B_WIDTH, D_MODEL), lambda t: (0, 0)),
            pl.BlockSpec((1, D_MODEL), lambda t: (0, 0)),
            pl.BlockSpec((tm, D_MODEL), lambda t: (t, 0)),
        ],
        out_specs=pl.BlockSpec((tm, D_MODEL), lambda t: (t, 0)),
        out_shape=jax.ShapeDtypeStruct(h.shape, h.dtype),
        compiler_params=_cparams(("parallel",)), name="out_proj",
    )(oa, ob, wa, wb, post_g, h)


def _pad_rows(x, rows):
    if x.shape[0] == rows:
        return x
    return jnp.concatenate([x, jnp.zeros((rows - x.shape[0], x.shape[1]), x.dtype)], axis=0)


def _meta_kv(km_ref, vm_ref):
    kmp = _pad_rows(km_ref[...], LANES)
    vmt = _pad_rows(vm_ref[...].astype(jnp.float32), LANES).T.astype(jnp.bfloat16)
    return kmp, vmt


def _diff_attn_kernel(lq1_ref, lk1_ref, lq2_ref, lk2_ref, sg_ref,
                      q_ref, k_ref, km_ref, vt_ref, vm_ref, gate_ref, o_ref,
                      qcat_sc, m_sc, l_sc, acc_sc, s_sc, mx_sc, sm_sc, mxm_sc,
                      *, n_chunks, ck, n_units, nq, n_qblocks, lambda_init):
    f32 = jnp.float32
    bq = n_units * nq
    lam = (jnp.exp(jnp.sum(lq1_ref[...] * lk1_ref[...], axis=-1, keepdims=True))
           - jnp.exp(jnp.sum(lq2_ref[...] * lk2_ref[...], axis=-1, keepdims=True)) + lambda_init)
    sg = sg_ref[...]
    lane = lax.broadcasted_iota(jnp.int32, (1, LANES), 1)
    comp0 = lane < HEAD_DIM
    kmp, vmt = _meta_kv(km_ref, vm_ref)
    meta_valid = lax.broadcasted_iota(jnp.int32, (LANES, 1), 0) < N_META

    def block_rows(ref, qb):
        return ref[pl.ds(pl.multiple_of(qb * bq, bq), bq), :]

    def start_block(qb):
        qrows = block_rows(q_ref, qb)
        qcats = []
        for u in range(n_units):
            q = _pad_rows(qrows[u * nq:(u + 1) * nq, :], LANES)
            zero = jnp.zeros_like(q)
            qcats += [jnp.where(comp0, q, zero), jnp.where(comp0, zero, q)]
        qcat = jnp.concatenate(qcats, axis=0)
        qcat_sc[...] = qcat
        s = lax.dot_general(kmp, qcat, _NT, preferred_element_type=f32)
        s = jnp.where(meta_valid, s, NEG_INF)
        sm_sc[...] = s
        mxm_sc[...] = jnp.max(s, axis=0, keepdims=True)
        scores(0)

    def scores(c):
        s = lax.dot_general(k_ref[c * ck:(c + 1) * ck, :], qcat_sc[...], _NT,
                            preferred_element_type=f32)
        s_sc[c % 2] = s
        mx_sc[c % 2] = jnp.max(s, axis=0, keepdims=True)

    def softmax_pv(c):
        m_old = m_sc[...]
        m_new = jnp.maximum(m_old, mx_sc[c % 2])
        alpha = jnp.exp2(m_old - m_new)
        p = jnp.exp2(s_sc[c % 2] - m_new)
        l_sc[...] = alpha * l_sc[...] + jnp.sum(p, axis=0, keepdims=True)
        acc_sc[...] = alpha * acc_sc[...] + jnp.dot(vt_ref[c], p.astype(jnp.bfloat16),
                                                    preferred_element_type=f32)
        m_sc[...] = m_new

    def finish_block(qb):
        a_all = acc_sc[...] * (1.0 / l_sc[...])
        gate = block_rows(gate_ref, qb)
        outs = []
        for u in range(n_units):
            a = a_all[:, u * 2 * LANES:(u + 1) * 2 * LANES]
            o = a[:, :LANES] - lam * a[:, LANES:]
            ms = jnp.mean(o * o, axis=0, keepdims=True)
            y = (o * lax.rsqrt(ms + EPS) * sg) * (1.0 - lambda_init)
            yt = y.T[:nq, :]
            outs.append((yt * gate[u * nq:(u + 1) * nq, :].astype(f32)).astype(o_ref.dtype))
        o_ref[pl.ds(pl.multiple_of(qb * bq, bq), bq), :] = jnp.concatenate(outs, axis=0)

    def block(qb, carry):
        m = mxm_sc[...]
        p = jnp.exp2(sm_sc[...] - m)
        m_sc[...] = m
        l_sc[...] = jnp.sum(p, axis=0, keepdims=True)
        acc_sc[...] = jnp.dot(vmt, p.astype(jnp.bfloat16), preferred_element_type=f32)
        for c in range(n_chunks):
            if c + 1 < n_chunks:
                scores(c + 1)
            softmax_pv(c)
        start_block(jnp.minimum(qb + 1, n_qblocks - 1))
        finish_block(qb)
        return carry

    start_block(0)
    lax.fori_loop(0, n_qblocks, block, 0)


def _diff_attn(lams, sg, q, k, km, vt, vm, gate, *, bsz, seq, tm, q_block, meta_q, lambda_init):
    n_chunks = seq // tm
    if meta_q:
        nq, n_units, n_qblocks = N_META, 1, 1
    else:
        nq, n_units, n_qblocks = LANES, q_block // LANES, seq // q_block
    q_rows = nq * n_units * n_qblocks
    width = n_units * 2 * LANES
    kernel = functools.partial(_diff_attn_kernel, n_chunks=n_chunks, ck=tm, n_units=n_units, nq=nq,
                               n_qblocks=n_qblocks, lambda_init=lambda_init)
    lam_spec = pl.BlockSpec((1, HEAD_DIM), lambda b, h: (0, 0))
    bh = lambda b, h: (b, h)
    return pl.pallas_call(
        kernel,
        grid=(bsz, B_HEADS),
        in_specs=[
            lam_spec, lam_spec, lam_spec, lam_spec,
            pl.BlockSpec((B_VDIM, 1), lambda b, h: (0, 0)),
            pl.BlockSpec((q_rows, LANES), bh),
            pl.BlockSpec((seq, LANES), bh),
            pl.BlockSpec((N_META, LANES), bh),
            pl.BlockSpec((n_chunks, B_VDIM, tm), lambda b, h: (b, h, 0)),
            pl.BlockSpec((N_META, B_VDIM), bh),
            pl.BlockSpec((q_rows, B_VDIM), bh),
        ],
        out_specs=pl.BlockSpec((q_rows, B_VDIM), bh),
        out_shape=jax.ShapeDtypeStruct((bsz * q_rows, B_WIDTH), jnp.bfloat16),
        scratch_shapes=[
            pltpu.VMEM((width, LANES), jnp.bfloat16),
            pltpu.VMEM((1, width), jnp.float32),
            pltpu.VMEM((1, width), jnp.float32),
            pltpu.VMEM((B_VDIM, width), jnp.float32),
            pltpu.VMEM((2, tm, width), jnp.float32),
            pltpu.VMEM((2, 1, width), jnp.float32),
            pltpu.VMEM((LANES, width), jnp.float32),
            pltpu.VMEM((1, width), jnp.float32),
        ],
        compiler_params=_cparams(("parallel", "arbitrary")),
        name="diff_attn_meta" if meta_q else "diff_attn",
    )(*lams, sg, q, k, km, vt, vm, gate)


def _window_attn_kernel(sink_ref, q_ref, k_ref, ks_ref, km_ref, kms_ref, vt_ref, vm_ref, gate_ref, o_ref,
                        s_sc, mx_sc, *, nb, nq, blocks_per_step, meta_q):
    f32 = jnp.float32
    bf = jnp.bfloat16
    step = pl.program_id(1)
    lane = lax.broadcasted_iota(jnp.int32, (1, LANES), 1)
    lo = lane < HEAD_DIM
    zero = jnp.zeros((LANES, LANES), bf)
    heads = (0, 2, 5, 7, 1, 3, 4, 6)

    kmp, vmt = _meta_kv(km_ref, vm_ref)
    kmsp = _pad_rows(kms_ref[...], LANES)
    key_i = lax.broadcasted_iota(jnp.int32, (LANES, LANES), 0)
    qry_i = lax.broadcasted_iota(jnp.int32, (LANES, LANES), 1)
    meta_mask = key_i < N_META
    sink_row = jnp.concatenate([jnp.full((1, LANES), sink_ref[hd] * LOG2E, f32) for hd in heads], axis=1)

    def rows_of(ref, blk):
        return ref[pl.ds(pl.multiple_of(blk * BLOCK, BLOCK), BLOCK), :]

    def pieces(n):
        if meta_q:
            return ([kmp, k_ref[0:BLOCK, :]], [kmsp, ks_ref[0:BLOCK, :]], [vmt, vt_ref[0]],
                    [meta_mask, (N_META + key_i - qry_i) <= WINDOW])
        prev = jnp.maximum(n - 1, 0)
        nxt = jnp.minimum(n + 1, nb - 1)
        return ([kmp, rows_of(k_ref, prev), rows_of(k_ref, n), rows_of(k_ref, nxt)],
                [kmsp, rows_of(ks_ref, prev), rows_of(ks_ref, n), rows_of(ks_ref, nxt)],
                [vmt, vt_ref[prev], vt_ref[n], vt_ref[nxt]],
                [meta_mask, (key_i >= qry_i) & (n > 0), None, (key_i <= qry_i) & (n < nb - 1)])

    def scores(j):
        n = step * blocks_per_step + j
        q = _pad_rows(q_ref[j * nq:(j + 1) * nq, :], LANES)
        cols = [q[:, c * LANES:(c + 1) * LANES] for c in range(A_Q // LANES)]
        lo_of = [jnp.where(lo, c, zero) for c in cols]
        hi_of = [jnp.where(lo, zero, c) for c in cols]
        q_plain = jnp.concatenate([lo_of[0], lo_of[1], hi_of[2], hi_of[3]], axis=0)
        q_swap = jnp.concatenate([hi_of[0], hi_of[1], lo_of[2], lo_of[3]], axis=0)
        pk, pks, _, masks = pieces(n)
        s = jnp.concatenate(
            [lax.dot_general(jnp.concatenate(pk, axis=0), q_plain, _NT, preferred_element_type=f32),
             lax.dot_general(jnp.concatenate(pks, axis=0), q_swap, _NT, preferred_element_type=f32)],
            axis=1)
        cols = []
        for t in range(len(heads)):
            sh = s[:, t * LANES:(t + 1) * LANES]
            rows = [sh[r * LANES:(r + 1) * LANES, :] if mask is None
                    else jnp.where(mask, sh[r * LANES:(r + 1) * LANES, :], NEG_INF)
                    for r, mask in enumerate(masks)]
            cols.append(jnp.concatenate(rows, axis=0))
        s = jnp.concatenate(cols, axis=1)
        s_sc[j % 2] = s
        mx_sc[j % 2] = jnp.max(s, axis=0, keepdims=True)

    def softmax_pv(j):
        n = step * blocks_per_step + j
        _, _, pvt, _ = pieces(n)
        m = jnp.maximum(mx_sc[j % 2], sink_row)
        p = jnp.exp2(s_sc[j % 2] - m)
        l = jnp.sum(p, axis=0, keepdims=True) + jnp.exp2(sink_row - m)
        ot = jnp.dot(jnp.concatenate(pvt, axis=1), p.astype(bf), preferred_element_type=f32)
        ot = ot * (1.0 / l)
        out_heads = [None] * A_HEADS
        for t, hd in enumerate(heads):
            g = hd // (A_HEADS // A_KV_HEADS)
            out_heads[hd] = ot[g * HEAD_DIM:(g + 1) * HEAD_DIM, t * LANES:(t + 1) * LANES]
        o = jnp.concatenate(out_heads, axis=0).T[:nq, :]
        o_ref[j * nq:(j + 1) * nq, :] = (o * gate_ref[j * nq:(j + 1) * nq, :].astype(f32)).astype(o_ref.dtype)

    scores(0)
    for j in range(blocks_per_step):
        if j + 1 < blocks_per_step:
            scores(j + 1)
        softmax_pv(j)


def _window_attn(sink, q, k, ks, km, kms, vt, vm, gate, *, bsz, seq, blocks_per_step, meta_q):
    nb = seq // BLOCK
    if meta_q:
        nq, bps, n_steps, n_keys = N_META, 1, 1, 2 * LANES
        k_spec = pl.BlockSpec((BLOCK, A_KV), lambda b, n: (b * nb, 0))
        vt_spec = pl.BlockSpec((1, A_KV, LANES), lambda b, n: (b * nb, 0, 0))
    else:
        bps = math.gcd(blocks_per_step, nb)
        nq, n_steps, n_keys = BLOCK, nb // bps, 4 * LANES
        k_spec = pl.BlockSpec((seq, A_KV), lambda b, n: (b, 0))
        vt_spec = pl.BlockSpec((nb, A_KV, LANES), lambda b, n: (b, 0, 0))
    q_index = lambda b, n: (b * n_steps + n, 0)
    meta_spec = pl.BlockSpec((N_META, A_KV), lambda b, n: (b, 0))
    return pl.pallas_call(
        functools.partial(_window_attn_kernel, nb=nb, nq=nq, blocks_per_step=bps, meta_q=meta_q),
        grid=(bsz, n_steps),
        in_specs=[
            pl.BlockSpec(memory_space=pltpu.SMEM),
            pl.BlockSpec((bps * nq, A_Q), q_index),
            k_spec, k_spec, meta_spec, meta_spec, vt_spec, meta_spec,
            pl.BlockSpec((bps * nq, A_WIDTH), q_index),
        ],
        out_specs=pl.BlockSpec((bps * nq, A_WIDTH), q_index),
        out_shape=jax.ShapeDtypeStruct((bsz * n_steps * bps * nq, A_WIDTH), jnp.bfloat16),
        scratch_shapes=[
            pltpu.VMEM((2, n_keys, A_HEADS * LANES), jnp.float32),
            pltpu.VMEM((2, 1, A_HEADS * LANES), jnp.float32),
        ],
        compiler_params=_cparams(("parallel", "arbitrary")),
        name="window_attn_meta" if meta_q else "window_attn",
    )(sink, q, k, ks, km, kms, vt, vm, gate)


def _rope_tables(pos):
    inv_freq = 1.0 / (ROPE_THETA ** (jnp.arange(0, HEAD_DIM, 2, dtype=jnp.float32) / HEAD_DIM))
    ang = pos.astype(jnp.float32)[:, None] * inv_freq[None, :]
    ang = jnp.concatenate([ang, ang, ang, ang], axis=-1)
    sign = jnp.where((jnp.arange(LANES) % HEAD_DIM) < HEAD_DIM // 2, -1.0, 1.0).astype(jnp.float32)
    return jnp.cos(ang), jnp.sin(ang) * sign[None, :]


def _encode_both(x_prompt, x_sample, meta_tokens, w_in, w_out, pre_norm_g, post_norm_g, sink_logits,
                 lambda_q1, lambda_k1, lambda_q2, lambda_k2, subln_g, *, tm, q_block):
    d = x_prompt.shape[-1]
    depth = w_in.shape[0]
    assert d == D_MODEL and tm % LANES == 0 and q_block % LANES == 0
    groups = []
    for x in (x_prompt, x_sample):
        bsz, seq, _ = x.shape
        assert seq % tm == 0 and seq % q_block == 0
        groups.append(dict(
            bsz=bsz, seq=seq,
            h_real=x.reshape(bsz * seq, d),
            h_meta=jnp.broadcast_to(meta_tokens[None], (bsz, N_META, d)).reshape(bsz * N_META, d),
            tab_real=_rope_tables(N_META + jnp.arange(seq)),
            tab_meta=_rope_tables(jnp.tile(jnp.arange(N_META), bsz)),
        ))

    for l in range(depth):
        lambda_init = 0.8 - 0.6 * math.exp(-0.3 * l)
        w_i = w_in[l].astype(jnp.bfloat16)
        w_o = w_out[l].astype(jnp.bfloat16)
        pre_g = pre_norm_g[l][None, :]
        post_g = post_norm_g[l][None, :]
        lams = (lambda_q1[l][None, :], lambda_k1[l][None, :], lambda_q2[l][None, :], lambda_k2[l][None, :])
        sg = subln_g[l][:, None]
        for grp in groups:
            bsz, seq = grp["bsz"], grp["seq"]
            (aq, ak, aks, avt, ag, bq, bk, bvt, bg) = _in_proj(
                grp["h_real"], pre_g, w_i, *grp["tab_real"], tm, True)
            (maq, mak, maks, mav, mag, mbq, mbk, mbv, mbg) = _in_proj(
                grp["h_meta"], pre_g, w_i, *grp["tab_meta"], bsz * N_META, False)
            win = functools.partial(_window_attn, sink_logits[l], bsz=bsz, seq=seq,
                                    blocks_per_step=WINDOW_BLOCKS_PER_STEP)
            oa = win(aq, ak, aks, mak, maks, avt, mav, ag, meta_q=False)
            moa = win(maq, ak, aks, mak, maks, avt, mav, mag, meta_q=True)
            dif = functools.partial(_diff_attn, lams, sg, bsz=bsz, seq=seq, tm=tm, q_block=q_block,
                                    lambda_init=lambda_init)
            ob = dif(bq, bk, mbk, bvt, mbv, bg, meta_q=False)
            mob = dif(mbq, bk, mbk, bvt, mbv, mbg, meta_q=True)
            grp["h_real"] = _out_proj(oa, ob, w_o[:A_WIDTH], w_o[A_WIDTH:], post_g, grp["h_real"], tm)
            grp["h_meta"] = _out_proj(moa, mob, w_o[:A_WIDTH], w_o[A_WIDTH:], post_g, grp["h_meta"],
                                      bsz * N_META)

    return tuple(g["h_real"].reshape(g["bsz"], g["seq"], d) for g in groups)


def kernel(x_prompt, x_sample, meta_tokens, w_in, w_out, pre_norm_g, post_norm_g, sink_logits,
           lambda_q1, lambda_k1, lambda_q2, lambda_k2, subln_g):
    return _encode_both(x_prompt, x_sample, meta_tokens, w_in, w_out, pre_norm_g, post_norm_g, sink_logits,
                        lambda_q1, lambda_k1, lambda_q2, lambda_k2, subln_g,
                        tm=ROW_TILE, q_block=DIFF_Q_BLOCK)
```

```python
import functools
import math

import jax
import jax.numpy as jnp
from jax import lax
from jax.experimental import pallas as pl
from jax.experimental.pallas import tpu as pltpu

D_MODEL = 1024
N_META = 16
HEAD_DIM = 64
BLOCK = 128
WINDOW = 128
ROPE_THETA = 10000.0
EPS = 1e-6
NEG_INF = -1e30

A_WIDTH = 512
A_HEADS = 8
A_KV_HEADS = 2
B_WIDTH = 512
B_VDIM = 128
B_HEADS = 4
A_Q = A_HEADS * HEAD_DIM
A_KV = A_KV_HEADS * HEAD_DIM
B_QK = 2 * B_HEADS * HEAD_DIM
IN_WIDTH = A_Q + 2 * A_KV + A_WIDTH + 2 * B_QK + 2 * B_WIDTH

OFF_AQ = 0
OFF_AK = OFF_AQ + A_Q
OFF_AV = OFF_AK + A_KV
OFF_AG = OFF_AV + A_KV
OFF_BQ = OFF_AG + A_WIDTH
OFF_BK = OFF_BQ + B_QK
OFF_BV = OFF_BK + B_QK
OFF_BG = OFF_BV + B_WIDTH

LANES = 128
LOG2E = math.log2(math.e)
Q_SCALE = HEAD_DIM ** -0.5 * LOG2E

ROW_TILE = 512
DIFF_Q_BLOCK = 512
DIFF_TILES_PER_CHUNK = 1
SUM_ROWS = 16
WINDOW_BLOCKS_PER_STEP = 8
VMEM_LIMIT = 56 * 1024 * 1024

_NT = (((1,), (1,)), ((), ()))


def _cparams(sem):
    return pltpu.CompilerParams(dimension_semantics=sem, vmem_limit_bytes=VMEM_LIMIT)


def _in_proj_kernel(h_ref, g_ref, w_ref, cos_ref, sin_ref,
                    aq_ref, ak_ref, aks_ref, av_ref, ag_ref, bq_ref, bk_ref, bv_ref, bg_ref,
                    *, transposed_v):
    x = h_ref[...]
    ms = jnp.mean(x * x, axis=-1, keepdims=True)
    hn = (x * lax.rsqrt(ms + EPS) * g_ref[...]).astype(jnp.bfloat16)
    cos = cos_ref[...]
    sin = sin_ref[...]
    lane = lax.broadcasted_iota(jnp.int32, (1, LANES), 1)
    first_half = (lane % HEAD_DIM) < (HEAD_DIM // 2)

    def proj(off, width):
        return jnp.dot(hn, w_ref[:, off:off + width], preferred_element_type=jnp.float32)

    def rope(p):
        fwd = pltpu.roll(p, LANES - HEAD_DIM // 2, 1)
        bwd = pltpu.roll(p, HEAD_DIM // 2, 1)
        return p * cos + jnp.where(first_half, fwd, bwd) * sin

    def silu(p):
        return p * (1.0 / (1.0 + jnp.exp(-p)))

    tm = x.shape[0]

    p = proj(OFF_AQ, A_Q)
    for j in range(A_Q // LANES):
        sl = slice(j * LANES, (j + 1) * LANES)
        aq_ref[:, sl] = (rope(p[:, sl]) * Q_SCALE).astype(jnp.bfloat16)

    p = proj(OFF_AK, 2 * A_KV)
    k = rope(p[:, :A_KV])
    ak_ref[...] = k.astype(jnp.bfloat16)
    aks_ref[...] = pltpu.roll(k, HEAD_DIM, 1).astype(jnp.bfloat16)
    v = p[:, A_KV:]
    if transposed_v:
        vt = v.T
        for j in range(tm // LANES):
            av_ref[j] = vt[:, j * LANES:(j + 1) * LANES].astype(jnp.bfloat16)
    else:
        av_ref[...] = v.astype(jnp.bfloat16)

    ag_ref[...] = silu(proj(OFF_AG, A_WIDTH)).astype(jnp.bfloat16)

    p = proj(OFF_BQ, B_QK)
    for j in range(B_QK // LANES):
        sl = slice(j * LANES, (j + 1) * LANES)
        bq_ref[:, sl] = (rope(p[:, sl]) * Q_SCALE).astype(jnp.bfloat16)

    p = proj(OFF_BK, B_QK)
    for j in range(B_QK // LANES):
        sl = slice(j * LANES, (j + 1) * LANES)
        bk_ref[:, sl] = rope(p[:, sl]).astype(jnp.bfloat16)

    p = proj(OFF_BV, B_WIDTH)
    if transposed_v:
        bv_ref[0] = p.T.astype(jnp.bfloat16)
    else:
        bv_ref[...] = p.astype(jnp.bfloat16)

    bg_ref[...] = silu(proj(OFF_BG, B_WIDTH)).astype(jnp.bfloat16)


def _in_proj(h, pre_g, w_bf16, cos_tab, sin_tab, tm, transposed_v):
    rows = h.shape[0]
    nt = rows // tm
    n_tab = cos_tab.shape[0] // tm
    bf = jnp.bfloat16

    def row_spec(width):
        return pl.BlockSpec((tm, width), lambda t: (t, 0))

    def row_shape(width):
        return jax.ShapeDtypeStruct((rows, width), bf)

    if transposed_v:
        av_shape = jax.ShapeDtypeStruct((rows // LANES, A_KV, LANES), bf)
        av_spec = pl.BlockSpec((tm // LANES, A_KV, LANES), lambda t: (t, 0, 0))
        bv_shape = jax.ShapeDtypeStruct((nt, B_WIDTH, tm), bf)
        bv_spec = pl.BlockSpec((1, B_WIDTH, tm), lambda t: (t, 0, 0))
    else:
        av_shape, av_spec = row_shape(A_KV), row_spec(A_KV)
        bv_shape, bv_spec = row_shape(B_WIDTH), row_spec(B_WIDTH)
    out_shape = (row_shape(A_Q), row_shape(A_KV), row_shape(A_KV), av_shape, row_shape(A_WIDTH),
                 row_shape(B_QK), row_shape(B_QK), bv_shape, row_shape(B_WIDTH))
    out_specs = (row_spec(A_Q), row_spec(A_KV), row_spec(A_KV), av_spec, row_spec(A_WIDTH),
                 row_spec(B_QK), row_spec(B_QK), bv_spec, row_spec(B_WIDTH))
    tab_spec = pl.BlockSpec((tm, LANES), lambda t: (t % n_tab, 0))
    return pl.pallas_call(
        functools.partial(_in_proj_kernel, transposed_v=transposed_v),
        grid=(nt,),
        in_specs=[
            row_spec(D_MODEL),
            pl.BlockSpec((1, D_MODEL), lambda t: (0, 0)),
            pl.BlockSpec((D_MODEL, IN_WIDTH), lambda t: (0, 0)),
            tab_spec, tab_spec,
        ],
        out_specs=out_specs, out_shape=out_shape,
        compiler_params=_cparams(("parallel",)), name="in_proj" if transposed_v else "in_proj_meta",
    )(h, pre_g, w_bf16, cos_tab, sin_tab)


def _out_proj_kernel(oa_ref, ob_ref, wa_ref, wb_ref, g_ref, h_ref, out_ref):
    o = jnp.dot(oa_ref[...], wa_ref[...], preferred_element_type=jnp.float32)
    o = o + jnp.dot(ob_ref[...], wb_ref[...], preferred_element_type=jnp.float32)
    ms = jnp.mean(o * o, axis=-1, keepdims=True)
    out_ref[...] = h_ref[...] + o * lax.rsqrt(ms + EPS) * g_ref[...]


def _out_proj(oa, ob, wa, wb, post_g, h, tm):
    rows = h.shape[0]
    return pl.pallas_call(
        _out_proj_kernel,
        grid=(rows // tm,),
        in_specs=[
            pl.BlockSpec((tm, A_WIDTH), lambda t: (t, 0)),
            pl.BlockSpec((tm, B_WIDTH), lambda t: (t, 0)),
            pl.BlockSpec((A_WIDTH, D_MODEL), lambda t: (0, 0)),
            pl.BlockSpec((B_WIDTH, D_MODEL), lambda t: (0, 0)),
            pl.BlockSpec((1, D_MODEL), lambda t: (0, 0)),
            pl.BlockSpec((tm, D_MODEL), lambda t: (t, 0)),
        ],
        out_specs=pl.BlockSpec((tm, D_MODEL), lambda t: (t, 0)),
        out_shape=jax.ShapeDtypeStruct(h.shape, h.dtype),
        compiler_params=_cparams(("parallel",)), name="out_proj",
    )(oa, ob, wa, wb, post_g, h)


def _pad_rows(x, rows):
    if x.shape[0] == rows:
        return x
    return jnp.concatenate([x, jnp.zeros((rows - x.shape[0], x.shape[1]), x.dtype)], axis=0)


def _meta_kv(km_ref, vm_ref):
    kmp = _pad_rows(km_ref[...], LANES)
    vmt = _pad_rows(vm_ref[...].astype(jnp.float32), LANES).T.astype(jnp.bfloat16)
    return kmp, vmt


def _diff_attn_kernel(lq1_ref, lk1_ref, lq2_ref, lk2_ref, sg_ref,
                      q_ref, k_ref, km_ref, vt_ref, vm_ref, gate_ref, o_ref,
                      qcat_sc, m_sc, acc_sc, s_sc, mx_sc, sm_sc, mxm_sc, p_sc, alpha_sc,
                      *, n_chunks, tiles_per_chunk, n_units, nq, n_qblocks, lambda_init):
    f32 = jnp.float32
    bq = n_units * nq
    ck = tiles_per_chunk * vt_ref.shape[2]
    lam = (jnp.exp(jnp.sum(lq1_ref[...] * lk1_ref[...], axis=-1, keepdims=True))
           - jnp.exp(jnp.sum(lq2_ref[...] * lk2_ref[...], axis=-1, keepdims=True)) + lambda_init)
    sg = sg_ref[...]
    lane = lax.broadcasted_iota(jnp.int32, (1, LANES), 1)
    comp0 = lane < HEAD_DIM
    kmp, vmt = _meta_kv(km_ref, vm_ref)
    meta_valid = lax.broadcasted_iota(jnp.int32, (LANES, 1), 0) < N_META

    def with_ones(vt):
        return jnp.concatenate([vt, jnp.ones((SUM_ROWS, vt.shape[1]), vt.dtype)], axis=0)

    def block_rows(ref, qb):
        return ref[pl.ds(pl.multiple_of(qb * bq, bq), bq), :]

    def start_block(qb):
        qrows = block_rows(q_ref, qb)
        qcats = []
        for u in range(n_units):
            q = _pad_rows(qrows[u * nq:(u + 1) * nq, :], LANES)
            zero = jnp.zeros_like(q)
            qcats += [jnp.where(comp0, q, zero), jnp.where(comp0, zero, q)]
        qcat = jnp.concatenate(qcats, axis=0)
        qcat_sc[...] = qcat
        s = lax.dot_general(kmp, qcat, _NT, preferred_element_type=f32)
        s = jnp.where(meta_valid, s, NEG_INF)
        sm_sc[...] = s
        mxm_sc[...] = jnp.max(s, axis=0, keepdims=True)
        scores(0)

    def scores(c):
        s = lax.dot_general(k_ref[c * ck:(c + 1) * ck, :], qcat_sc[...], _NT,
                            preferred_element_type=f32)
        s_sc[c % 2] = s
        mx_sc[c % 2] = jnp.max(s, axis=0, keepdims=True)

    def softmax(c):
        m_old = m_sc[...]
        m_new = jnp.maximum(m_old, mx_sc[c % 2])
        alpha = jnp.exp2(m_old - m_new)
        p_sc[c % 2] = jnp.exp2(s_sc[c % 2] - m_new).astype(jnp.bfloat16)
        alpha_sc[c % 2] = alpha
        m_sc[...] = m_new

    def pv(c):
        vt = jnp.concatenate([vt_ref[c * tiles_per_chunk + i] for i in range(tiles_per_chunk)], axis=1)
        acc_sc[...] = alpha_sc[c % 2] * acc_sc[...] + jnp.dot(with_ones(vt), p_sc[c % 2],
                                                               preferred_element_type=f32)

    def finish_block(qb):
        a_all = acc_sc[:B_VDIM, :] * (1.0 / acc_sc[B_VDIM:B_VDIM + 1, :])
        gate = block_rows(gate_ref, qb)
        outs = []
        for u in range(n_units):
            a = a_all[:, u * 2 * LANES:(u + 1) * 2 * LANES]
            o = a[:, :LANES] - lam * a[:, LANES:]
            ms = jnp.mean(o * o, axis=0, keepdims=True)
            y = (o * lax.rsqrt(ms + EPS) * sg) * (1.0 - lambda_init)
            yt = y.T[:nq, :]
            outs.append((yt * gate[u * nq:(u + 1) * nq, :].astype(f32)).astype(o_ref.dtype))
        o_ref[pl.ds(pl.multiple_of(qb * bq, bq), bq), :] = jnp.concatenate(outs, axis=0)

    def block(qb, carry):
        m = mxm_sc[...]
        p = jnp.exp2(sm_sc[...] - m)
        m_sc[...] = m
        acc_sc[...] = jnp.dot(with_ones(vmt), p.astype(jnp.bfloat16), preferred_element_type=f32)
        for c in range(n_chunks):
            softmax(c)
            if c + 1 < n_chunks:
                scores(c + 1)
            if c > 0:
                pv(c - 1)
        pv(n_chunks - 1)
        start_block(jnp.minimum(qb + 1, n_qblocks - 1))
        finish_block(qb)
        return carry

    start_block(0)
    lax.fori_loop(0, n_qblocks, block, 0)


def _diff_attn(lams, sg, q, k, km, vt, vm, gate, *, bsz, seq, tm, q_block, tiles_per_chunk, meta_q,
               lambda_init):
    n_tiles = seq // tm
    tiles_per_chunk = math.gcd(tiles_per_chunk, n_tiles)
    ck = tiles_per_chunk * tm
    n_chunks = n_tiles // tiles_per_chunk
    if meta_q:
        nq, n_units, n_qblocks = N_META, 1, 1
    else:
        nq, n_units, n_qblocks = LANES, q_block // LANES, seq // q_block
    q_rows = nq * n_units * n_qblocks
    width = n_units * 2 * LANES
    kernel = functools.partial(_diff_attn_kernel, n_chunks=n_chunks, tiles_per_chunk=tiles_per_chunk,
                               n_units=n_units, nq=nq, n_qblocks=n_qblocks, lambda_init=lambda_init)
    lam_spec = pl.BlockSpec((1, HEAD_DIM), lambda b, h: (0, 0))
    bh = lambda b, h: (b, h)
    return pl.pallas_call(
        kernel,
        grid=(bsz, B_HEADS),
        in_specs=[
            lam_spec, lam_spec, lam_spec, lam_spec,
            pl.BlockSpec((B_VDIM, 1), lambda b, h: (0, 0)),
            pl.BlockSpec((q_rows, LANES), bh),
            pl.BlockSpec((seq, LANES), bh),
            pl.BlockSpec((N_META, LANES), bh),
            pl.BlockSpec((n_tiles, B_VDIM, tm), lambda b, h: (b, h, 0)),
            pl.BlockSpec((N_META, B_VDIM), bh),
            pl.BlockSpec((q_rows, B_VDIM), bh),
        ],
        out_specs=pl.BlockSpec((q_rows, B_VDIM), bh),
        out_shape=jax.ShapeDtypeStruct((bsz * q_rows, B_WIDTH), jnp.bfloat16),
        scratch_shapes=[
            pltpu.VMEM((width, LANES), jnp.bfloat16),
            pltpu.VMEM((1, width), jnp.float32),
            pltpu.VMEM((B_VDIM + SUM_ROWS, width), jnp.float32),
            pltpu.VMEM((2, ck, width), jnp.float32),
            pltpu.VMEM((2, 1, width), jnp.float32),
            pltpu.VMEM((LANES, width), jnp.float32),
            pltpu.VMEM((1, width), jnp.float32),
            pltpu.VMEM((2, ck, width), jnp.bfloat16),
            pltpu.VMEM((2, 1, width), jnp.float32),
        ],
        compiler_params=_cparams(("parallel", "arbitrary")),
        name="diff_attn_meta" if meta_q else "diff_attn",
    )(*lams, sg, q, k, km, vt, vm, gate)


def _window_attn_kernel(sink_ref, q_ref, k_ref, ks_ref, km_ref, kms_ref, vt_ref, vm_ref, gate_ref, o_ref,
                        s_sc, mx_sc, *, nb, nq, blocks_per_step, meta_q):
    f32 = jnp.float32
    bf = jnp.bfloat16
    step = pl.program_id(1)
    lane = lax.broadcasted_iota(jnp.int32, (1, LANES), 1)
    lo = lane < HEAD_DIM
    zero = jnp.zeros((LANES, LANES), bf)
    heads = (0, 2, 5, 7, 1, 3, 4, 6)

    kmp, vmt = _meta_kv(km_ref, vm_ref)
    kmsp = _pad_rows(kms_ref[...], LANES)
    key_i = lax.broadcasted_iota(jnp.int32, (LANES, LANES), 0)
    qry_i = lax.broadcasted_iota(jnp.int32, (LANES, LANES), 1)
    meta_mask = key_i < N_META
    sink_row = jnp.concatenate([jnp.full((1, LANES), sink_ref[hd] * LOG2E, f32) for hd in heads], axis=1)

    def rows_of(ref, blk):
        return ref[pl.ds(pl.multiple_of(blk * BLOCK, BLOCK), BLOCK), :]

    def pieces(n):
        if meta_q:
            return ([kmp, k_ref[0:BLOCK, :]], [kmsp, ks_ref[0:BLOCK, :]], [vmt, vt_ref[0]],
                    [meta_mask, (N_META + key_i - qry_i) <= WINDOW])
        prev = jnp.maximum(n - 1, 0)
        nxt = jnp.minimum(n + 1, nb - 1)
        return ([kmp, rows_of(k_ref, prev), rows_of(k_ref, n), rows_of(k_ref, nxt)],
                [kmsp, rows_of(ks_ref, prev), rows_of(ks_ref, n), rows_of(ks_ref, nxt)],
                [vmt, vt_ref[prev], vt_ref[n], vt_ref[nxt]],
                [meta_mask, (key_i >= qry_i) & (n > 0), None, (key_i <= qry_i) & (n < nb - 1)])

    def scores(j):
        n = step * blocks_per_step + j
        q = _pad_rows(q_ref[j * nq:(j + 1) * nq, :], LANES)
        cols = [q[:, c * LANES:(c + 1) * LANES] for c in range(A_Q // LANES)]
        lo_of = [jnp.where(lo, c, zero) for c in cols]
        hi_of = [jnp.where(lo, zero, c) for c in cols]
        q_plain = jnp.concatenate([lo_of[0], lo_of[1], hi_of[2], hi_of[3]], axis=0)
        q_swap = jnp.concatenate([hi_of[0], hi_of[1], lo_of[2], lo_of[3]], axis=0)
        pk, pks, _, masks = pieces(n)
        s = jnp.concatenate(
            [lax.dot_general(jnp.concatenate(pk, axis=0), q_plain, _NT, preferred_element_type=f32),
             lax.dot_general(jnp.concatenate(pks, axis=0), q_swap, _NT, preferred_element_type=f32)],
            axis=1)
        cols = []
        for t in range(len(heads)):
            sh = s[:, t * LANES:(t + 1) * LANES]
            rows = [sh[r * LANES:(r + 1) * LANES, :] if mask is None
                    else jnp.where(mask, sh[r * LANES:(r + 1) * LANES, :], NEG_INF)
                    for r, mask in enumerate(masks)]
            cols.append(jnp.concatenate(rows, axis=0))
        s = jnp.concatenate(cols, axis=1)
        s_sc[j % 2] = s
        mx_sc[j % 2] = jnp.max(s, axis=0, keepdims=True)

    def softmax_pv(j):
        n = step * blocks_per_step + j
        _, _, pvt, _ = pieces(n)
        m = jnp.maximum(mx_sc[j % 2], sink_row)
        p = jnp.exp2(s_sc[j % 2] - m)
        l = jnp.sum(p, axis=0, keepdims=True) + jnp.exp2(sink_row - m)
        ot = jnp.dot(jnp.concatenate(pvt, axis=1), p.astype(bf), preferred_element_type=f32)
        ot = ot * (1.0 / l)
        out_heads = [None] * A_HEADS
        for t, hd in enumerate(heads):
            g = hd // (A_HEADS // A_KV_HEADS)
            out_heads[hd] = ot[g * HEAD_DIM:(g + 1) * HEAD_DIM, t * LANES:(t + 1) * LANES]
        o = jnp.concatenate(out_heads, axis=0).T[:nq, :]
        o_ref[j * nq:(j + 1) * nq, :] = (o * gate_ref[j * nq:(j + 1) * nq, :].astype(f32)).astype(o_ref.dtype)

    scores(0)
    for j in range(blocks_per_step):
        if j + 1 < blocks_per_step:
            scores(j + 1)
        softmax_pv(j)


def _window_attn(sink, q, k, ks, km, kms, vt, vm, gate, *, bsz, seq, blocks_per_step, meta_q):
    nb = seq // BLOCK
    if meta_q:
        nq, bps, n_steps, n_keys = N_META, 1, 1, 2 * LANES
        k_spec = pl.BlockSpec((BLOCK, A_KV), lambda b, n: (b * nb, 0))
        vt_spec = pl.BlockSpec((1, A_KV, LANES), lambda b, n: (b * nb, 0, 0))
    else:
        bps = math.gcd(blocks_per_step, nb)
        nq, n_steps, n_keys = BLOCK, nb // bps, 4 * LANES
        k_spec = pl.BlockSpec((seq, A_KV), lambda b, n: (b, 0))
        vt_spec = pl.BlockSpec((nb, A_KV, LANES), lambda b, n: (b, 0, 0))
    q_index = lambda b, n: (b * n_steps + n, 0)
    meta_spec = pl.BlockSpec((N_META, A_KV), lambda b, n: (b, 0))
    return pl.pallas_call(
        functools.partial(_window_attn_kernel, nb=nb, nq=nq, blocks_per_step=bps, meta_q=meta_q),
        grid=(bsz, n_steps),
        in_specs=[
            pl.BlockSpec(memory_space=pltpu.SMEM),
            pl.BlockSpec((bps * nq, A_Q), q_index),
            k_spec, k_spec, meta_spec, meta_spec, vt_spec, meta_spec,
            pl.BlockSpec((bps * nq, A_WIDTH), q_index),
        ],
        out_specs=pl.BlockSpec((bps * nq, A_WIDTH), q_index),
        out_shape=jax.ShapeDtypeStruct((bsz * n_steps * bps * nq, A_WIDTH), jnp.bfloat16),
        scratch_shapes=[
            pltpu.VMEM((2, n_keys, A_HEADS * LANES), jnp.float32),
            pltpu.VMEM((2, 1, A_HEADS * LANES), jnp.float32),
        ],
        compiler_params=_cparams(("parallel", "arbitrary")),
        name="window_attn_meta" if meta_q else "window_attn",
    )(sink, q, k, ks, km, kms, vt, vm, gate)


def _rope_tables(pos):
    inv_freq = 1.0 / (ROPE_THETA ** (jnp.arange(0, HEAD_DIM, 2, dtype=jnp.float32) / HEAD_DIM))
    ang = pos.astype(jnp.float32)[:, None] * inv_freq[None, :]
    ang = jnp.concatenate([ang, ang, ang, ang], axis=-1)
    sign = jnp.where((jnp.arange(LANES) % HEAD_DIM) < HEAD_DIM // 2, -1.0, 1.0).astype(jnp.float32)
    return jnp.cos(ang), jnp.sin(ang) * sign[None, :]


def _encode_both(x_prompt, x_sample, meta_tokens, w_in, w_out, pre_norm_g, post_norm_g, sink_logits,
                 lambda_q1, lambda_k1, lambda_q2, lambda_k2, subln_g, *, tm, q_block):
    d = x_prompt.shape[-1]
    depth = w_in.shape[0]
    assert d == D_MODEL and tm % LANES == 0 and q_block % LANES == 0
    groups = []
    for x in (x_prompt, x_sample):
        bsz, seq, _ = x.shape
        assert seq % tm == 0 and seq % q_block == 0
        groups.append(dict(
            bsz=bsz, seq=seq,
            h_real=x.reshape(bsz * seq, d),
            h_meta=jnp.broadcast_to(meta_tokens[None], (bsz, N_META, d)).reshape(bsz * N_META, d),
            tab_real=_rope_tables(N_META + jnp.arange(seq)),
            tab_meta=_rope_tables(jnp.tile(jnp.arange(N_META), bsz)),
        ))

    for l in range(depth):
        lambda_init = 0.8 - 0.6 * math.exp(-0.3 * l)
        w_i = w_in[l].astype(jnp.bfloat16)
        w_o = w_out[l].astype(jnp.bfloat16)
        pre_g = pre_norm_g[l][None, :]
        post_g = post_norm_g[l][None, :]
        lams = (lambda_q1[l][None, :], lambda_k1[l][None, :], lambda_q2[l][None, :], lambda_k2[l][None, :])
        sg = subln_g[l][:, None]
        for grp in groups:
            bsz, seq = grp["bsz"], grp["seq"]
            (aq, ak, aks, avt, ag, bq, bk, bvt, bg) = _in_proj(
                grp["h_real"], pre_g, w_i, *grp["tab_real"], tm, True)
            (maq, mak, maks, mav, mag, mbq, mbk, mbv, mbg) = _in_proj(
                grp["h_meta"], pre_g, w_i, *grp["tab_meta"], bsz * N_META, False)
            win = functools.partial(_window_attn, sink_logits[l], bsz=bsz, seq=seq,
                                    blocks_per_step=WINDOW_BLOCKS_PER_STEP)
            oa = win(aq, ak, aks, mak, maks, avt, mav, ag, meta_q=False)
            moa = win(maq, ak, aks, mak, maks, avt, mav, mag, meta_q=True)
            dif = functools.partial(_diff_attn, lams, sg, bsz=bsz, seq=seq, tm=tm, q_block=q_block,
                                    tiles_per_chunk=DIFF_TILES_PER_CHUNK, lambda_init=lambda_init)
            ob = dif(bq, bk, mbk, bvt, mbv, bg, meta_q=False)
            mob = dif(mbq, bk, mbk, bvt, mbv, mbg, meta_q=True)
            grp["h_real"] = _out_proj(oa, ob, w_o[:A_WIDTH], w_o[A_WIDTH:], post_g, grp["h_real"], tm)
            grp["h_meta"] = _out_proj(moa, mob, w_o[:A_WIDTH], w_o[A_WIDTH:], post_g, grp["h_meta"],
                                      bsz * N_META)

    return tuple(g["h_real"].reshape(g["bsz"], g["seq"], d) for g in groups)


def kernel(x_prompt, x_sample, meta_tokens, w_in, w_out, pre_norm_g, post_norm_g, sink_logits,
           lambda_q1, lambda_k1, lambda_q2, lambda_k2, subln_g):
    return _encode_both(x_prompt, x_sample, meta_tokens, w_in, w_out, pre_norm_g, post_norm_g, sink_logits,
                        lambda_q1, lambda_k1, lambda_q2, lambda_k2, subln_g,
                        tm=ROW_TILE, q_block=DIFF_Q_BLOCK)
```

```python
import functools
import math

import jax
import jax.numpy as jnp
from jax import lax
from jax.experimental import pallas as pl
from jax.experimental.pallas import tpu as pltpu

D_MODEL = 1024
N_META = 16
HEAD_DIM = 64
BLOCK = 128
WINDOW = 128
ROPE_THETA = 10000.0
EPS = 1e-6
NEG_INF = -1e30

A_WIDTH = 512
A_HEADS = 8
A_KV_HEADS = 2
B_WIDTH = 512
B_VDIM = 128
B_HEADS = 4
A_Q = A_HEADS * HEAD_DIM
A_KV = A_KV_HEADS * HEAD_DIM
B_QK = 2 * B_HEADS * HEAD_DIM
IN_WIDTH = A_Q + 2 * A_KV + A_WIDTH + 2 * B_QK + 2 * B_WIDTH

OFF_AQ = 0
OFF_AK = OFF_AQ + A_Q
OFF_AV = OFF_AK + A_KV
OFF_AG = OFF_AV + A_KV
OFF_BQ = OFF_AG + A_WIDTH
OFF_BK = OFF_BQ + B_QK
OFF_BV = OFF_BK + B_QK
OFF_BG = OFF_BV + B_WIDTH

LANES = 128
LOG2E = math.log2(math.e)
Q_SCALE = HEAD_DIM ** -0.5 * LOG2E

ROW_TILE = 512
DIFF_Q_BLOCK = 512
DIFF_TILES_PER_CHUNK = 1
SUM_ROWS = 16
WINDOW_BLOCKS_PER_STEP = 8
VMEM_LIMIT = 56 * 1024 * 1024

_NT = (((1,), (1,)), ((), ()))


def _cparams(sem):
    return pltpu.CompilerParams(dimension_semantics=sem, vmem_limit_bytes=VMEM_LIMIT)


def _in_proj_kernel(h_ref, g_ref, w_ref, cos_ref, sin_ref,
                    aq_ref, ak_ref, aks_ref, av_ref, ag_ref, bq_ref, bk_ref, bv_ref, bg_ref,
                    *, transposed_v):
    x = h_ref[...]
    ms = jnp.mean(x * x, axis=-1, keepdims=True)
    hn = (x * lax.rsqrt(ms + EPS) * g_ref[...]).astype(jnp.bfloat16)
    cos = cos_ref[...]
    sin = sin_ref[...]
    lane = lax.broadcasted_iota(jnp.int32, (1, LANES), 1)
    first_half = (lane % HEAD_DIM) < (HEAD_DIM // 2)

    def proj(off, width):
        return jnp.dot(hn, w_ref[:, off:off + width], preferred_element_type=jnp.float32)

    def rope(p):
        fwd = pltpu.roll(p, LANES - HEAD_DIM // 2, 1)
        bwd = pltpu.roll(p, HEAD_DIM // 2, 1)
        return p * cos + jnp.where(first_half, fwd, bwd) * sin

    def silu(p):
        return p * (1.0 / (1.0 + jnp.exp(-p)))

    tm = x.shape[0]

    p = proj(OFF_AQ, A_Q)
    for j in range(A_Q // LANES):
        sl = slice(j * LANES, (j + 1) * LANES)
        aq_ref[:, sl] = (rope(p[:, sl]) * Q_SCALE).astype(jnp.bfloat16)

    p = proj(OFF_AK, 2 * A_KV)
    k = rope(p[:, :A_KV])
    ak_ref[...] = k.astype(jnp.bfloat16)
    aks_ref[...] = pltpu.roll(k, HEAD_DIM, 1).astype(jnp.bfloat16)
    v = p[:, A_KV:]
    if transposed_v:
        vt = v.T
        for j in range(tm // LANES):
            av_ref[j] = vt[:, j * LANES:(j + 1) * LANES].astype(jnp.bfloat16)
    else:
        av_ref[...] = v.astype(jnp.bfloat16)

    ag_ref[...] = silu(proj(OFF_AG, A_WIDTH)).astype(jnp.bfloat16)

    p = proj(OFF_BQ, B_QK)
    for j in range(B_QK // LANES):
        sl = slice(j * LANES, (j + 1) * LANES)
        bq_ref[:, sl] = (rope(p[:, sl]) * Q_SCALE).astype(jnp.bfloat16)

    p = proj(OFF_BK, B_QK)
    for j in range(B_QK // LANES):
        sl = slice(j * LANES, (j + 1) * LANES)
        bk_ref[:, sl] = rope(p[:, sl]).astype(jnp.bfloat16)

    p = proj(OFF_BV, B_WIDTH)
    if transposed_v:
        bv_ref[0] = p.T.astype(jnp.bfloat16)
    else:
        bv_ref[...] = p.astype(jnp.bfloat16)

    bg_ref[...] = silu(proj(OFF_BG, B_WIDTH)).astype(jnp.bfloat16)


def _in_proj(h, pre_g, w_bf16, cos_tab, sin_tab, tm, transposed_v):
    rows = h.shape[0]
    nt = rows // tm
    n_tab = cos_tab.shape[0] // tm
    bf = jnp.bfloat16

    def row_spec(width):
        return pl.BlockSpec((tm, width), lambda t: (t, 0))

    def row_shape(width):
        return jax.ShapeDtypeStruct((rows, width), bf)

    if transposed_v:
        av_shape = jax.ShapeDtypeStruct((rows // LANES, A_KV, LANES), bf)
        av_spec = pl.BlockSpec((tm // LANES, A_KV, LANES), lambda t: (t, 0, 0))
        bv_shape = jax.ShapeDtypeStruct((nt, B_WIDTH, tm), bf)
        bv_spec = pl.BlockSpec((1, B_WIDTH, tm), lambda t: (t, 0, 0))
    else:
        av_shape, av_spec = row_shape(A_KV), row_spec(A_KV)
        bv_shape, bv_spec = row_shape(B_WIDTH), row_spec(B_WIDTH)
    out_shape = (row_shape(A_Q), row_shape(A_KV), row_shape(A_KV), av_shape, row_shape(A_WIDTH),
                 row_shape(B_QK), row_shape(B_QK), bv_shape, row_shape(B_WIDTH))
    out_specs = (row_spec(A_Q), row_spec(A_KV), row_spec(A_KV), av_spec, row_spec(A_WIDTH),
                 row_spec(B_QK), row_spec(B_QK), bv_spec, row_spec(B_WIDTH))
    tab_spec = pl.BlockSpec((tm, LANES), lambda t: (t % n_tab, 0))
    return pl.pallas_call(
        functools.partial(_in_proj_kernel, transposed_v=transposed_v),
        grid=(nt,),
        in_specs=[
            row_spec(D_MODEL),
            pl.BlockSpec((1, D_MODEL), lambda t: (0, 0)),
            pl.BlockSpec((D_MODEL, IN_WIDTH), lambda t: (0, 0)),
            tab_spec, tab_spec,
        ],
        out_specs=out_specs, out_shape=out_shape,
        compiler_params=_cparams(("parallel",)), name="in_proj" if transposed_v else "in_proj_meta",
    )(h, pre_g, w_bf16, cos_tab, sin_tab)


def _out_proj_kernel(oa_ref, ob_ref, wa_ref, wb_ref, g_ref, h_ref, out_ref):
    o = jnp.dot(oa_ref[...], wa_ref[...], preferred_element_type=jnp.float32)
    o = o + jnp.dot(ob_ref[...], wb_ref[...], preferred_element_type=jnp.float32)
    ms = jnp.mean(o * o, axis=-1, keepdims=True)
    out_ref[...] = h_ref[...] + o * lax.rsqrt(ms + EPS) * g_ref[...]


def _out_proj(oa, ob, wa, wb, post_g, h, tm):
    rows = h.shape[0]
    return pl.pallas_call(
        _out_proj_kernel,
        grid=(rows // tm,),
        in_specs=[
            pl.BlockSpec((tm, A_WIDTH), lambda t: (t, 0)),
            pl.BlockSpec((tm, B_WIDTH), lambda t: (t, 0)),
            pl.BlockSpec((A_WIDTH, D_MODEL), lambda t: (0, 0)),
            pl.BlockSpec((B_WIDTH, D_MODEL), lambda t: (0, 0)),
            pl.BlockSpec((1, D_MODEL), lambda t: (0, 0)),
            pl.BlockSpec((tm, D_MODEL), lambda t: (t, 0)),
        ],
        out_specs=pl.BlockSpec((tm, D_MODEL), lambda t: (t, 0)),
        out_shape=jax.ShapeDtypeStruct(h.shape, h.dtype),
        compiler_params=_cparams(("parallel",)), name="out_proj",
    )(oa, ob, wa, wb, post_g, h)


def _pad_rows(x, rows):
    if x.shape[0] == rows:
        return x
    return jnp.concatenate([x, jnp.zeros((rows - x.shape[0], x.shape[1]), x.dtype)], axis=0)


def _meta_kv(km_ref, vm_ref):
    kmp = _pad_rows(km_ref[...], LANES)
    vmt = _pad_rows(vm_ref[...].astype(jnp.float32), LANES).T.astype(jnp.bfloat16)
    return kmp, vmt


def _diff_attn_kernel(lq1_ref, lk1_ref, lq2_ref, lk2_ref, sg_ref,
                      q_ref, k_ref, km_ref, vt_ref, vm_ref, gate_ref, o_ref,
                      qcat_sc, m_sc, acc_sc, s_sc, mx_sc, sm_sc, mxm_sc, p_sc, alpha_sc,
                      *, n_chunks, tiles_per_chunk, n_units, nq, n_qblocks, lambda_init):
    f32 = jnp.float32
    bq = n_units * nq
    ck = tiles_per_chunk * vt_ref.shape[2]
    lam = (jnp.exp(jnp.sum(lq1_ref[...] * lk1_ref[...], axis=-1, keepdims=True))
           - jnp.exp(jnp.sum(lq2_ref[...] * lk2_ref[...], axis=-1, keepdims=True)) + lambda_init)
    sg = sg_ref[...]
    lane = lax.broadcasted_iota(jnp.int32, (1, LANES), 1)
    comp0 = lane < HEAD_DIM
    kmp, vmt = _meta_kv(km_ref, vm_ref)
    meta_valid = lax.broadcasted_iota(jnp.int32, (LANES, 1), 0) < N_META

    def with_ones(vt):
        return jnp.concatenate([vt, jnp.ones((SUM_ROWS, vt.shape[1]), vt.dtype)], axis=0)

    def block_rows(ref, qb):
        return ref[pl.ds(pl.multiple_of(qb * bq, bq), bq), :]

    def start_block(qb):
        qrows = block_rows(q_ref, qb)
        qcats = []
        for u in range(n_units):
            q = _pad_rows(qrows[u * nq:(u + 1) * nq, :], LANES)
            zero = jnp.zeros_like(q)
            qcats += [jnp.where(comp0, q, zero), jnp.where(comp0, zero, q)]
        qcat = jnp.concatenate(qcats, axis=0)
        qcat_sc[...] = qcat
        s = lax.dot_general(kmp, qcat, _NT, preferred_element_type=f32)
        s = jnp.where(meta_valid, s, NEG_INF)
        sm_sc[...] = s
        mxm_sc[...] = jnp.max(s, axis=0, keepdims=True)
        scores(0)

    def scores(c):
        s = lax.dot_general(k_ref[c * ck:(c + 1) * ck, :], qcat_sc[...], _NT,
                            preferred_element_type=f32)
        s_sc[c % 2] = s
        mx_sc[c % 2] = jnp.max(s, axis=0, keepdims=True)

    def softmax(c):
        m_old = m_sc[...]
        m_new = jnp.maximum(m_old, mx_sc[c % 2])
        alpha = jnp.exp2(m_old - m_new)
        p_sc[c % 2] = jnp.exp2(s_sc[c % 2] - m_new).astype(jnp.bfloat16)
        alpha_sc[c % 2] = alpha
        m_sc[...] = m_new

    def pv(c):
        vt = jnp.concatenate([vt_ref[c * tiles_per_chunk + i] for i in range(tiles_per_chunk)], axis=1)
        acc_sc[...] = alpha_sc[c % 2] * acc_sc[...] + jnp.dot(with_ones(vt), p_sc[c % 2],
                                                               preferred_element_type=f32)

    def finish_block(qb):
        a_all = acc_sc[:B_VDIM, :] * (1.0 / acc_sc[B_VDIM:B_VDIM + 1, :])
        gate = block_rows(gate_ref, qb)
        outs = []
        for u in range(n_units):
            a = a_all[:, u * 2 * LANES:(u + 1) * 2 * LANES]
            o = a[:, :LANES] - lam * a[:, LANES:]
            ms = jnp.mean(o * o, axis=0, keepdims=True)
            y = (o * lax.rsqrt(ms + EPS) * sg) * (1.0 - lambda_init)
            yt = y.T[:nq, :]
            outs.append((yt * gate[u * nq:(u + 1) * nq, :].astype(f32)).astype(o_ref.dtype))
        o_ref[pl.ds(pl.multiple_of(qb * bq, bq), bq), :] = jnp.concatenate(outs, axis=0)

    def block(qb, carry):
        m = mxm_sc[...]
        p = jnp.exp2(sm_sc[...] - m)
        m_sc[...] = m
        acc_sc[...] = jnp.dot(with_ones(vmt), p.astype(jnp.bfloat16), preferred_element_type=f32)
        for c in range(n_chunks):
            softmax(c)
            if c + 1 < n_chunks:
                scores(c + 1)
            if c > 0:
                pv(c - 1)
        pv(n_chunks - 1)
        start_block(jnp.minimum(qb + 1, n_qblocks - 1))
        finish_block(qb)
        return carry

    start_block(0)
    lax.fori_loop(0, n_qblocks, block, 0)


def _diff_attn(lams, sg, q, k, km, vt, vm, gate, *, bsz, seq, tm, q_block, tiles_per_chunk, meta_q,
               lambda_init):
    n_tiles = seq // tm
    tiles_per_chunk = math.gcd(tiles_per_chunk, n_tiles)
    ck = tiles_per_chunk * tm
    n_chunks = n_tiles // tiles_per_chunk
    if meta_q:
        nq, n_units, n_qblocks = N_META, 1, 1
    else:
        nq, n_units, n_qblocks = LANES, q_block // LANES, seq // q_block
    q_rows = nq * n_units * n_qblocks
    width = n_units * 2 * LANES
    kernel = functools.partial(_diff_attn_kernel, n_chunks=n_chunks, tiles_per_chunk=tiles_per_chunk,
                               n_units=n_units, nq=nq, n_qblocks=n_qblocks, lambda_init=lambda_init)
    lam_spec = pl.BlockSpec((1, HEAD_DIM), lambda b, h: (0, 0))
    bh = lambda b, h: (b, h)
    return pl.pallas_call(
        kernel,
        grid=(bsz, B_HEADS),
        in_specs=[
            lam_spec, lam_spec, lam_spec, lam_spec,
            pl.BlockSpec((B_VDIM, 1), lambda b, h: (0, 0)),
            pl.BlockSpec((q_rows, LANES), bh),
            pl.BlockSpec((seq, LANES), bh),
            pl.BlockSpec((N_META, LANES), bh),
            pl.BlockSpec((n_tiles, B_VDIM, tm), lambda b, h: (b, h, 0)),
            pl.BlockSpec((N_META, B_VDIM), bh),
            pl.BlockSpec((q_rows, B_VDIM), bh),
        ],
        out_specs=pl.BlockSpec((q_rows, B_VDIM), bh),
        out_shape=jax.ShapeDtypeStruct((bsz * q_rows, B_WIDTH), jnp.bfloat16),
        scratch_shapes=[
            pltpu.VMEM((width, LANES), jnp.bfloat16),
            pltpu.VMEM((1, width), jnp.float32),
            pltpu.VMEM((B_VDIM + SUM_ROWS, width), jnp.float32),
            pltpu.VMEM((2, ck, width), jnp.float32),
            pltpu.VMEM((2, 1, width), jnp.float32),
            pltpu.VMEM((LANES, width), jnp.float32),
            pltpu.VMEM((1, width), jnp.float32),
            pltpu.VMEM((2, ck, width), jnp.bfloat16),
            pltpu.VMEM((2, 1, width), jnp.float32),
        ],
        compiler_params=_cparams(("parallel", "arbitrary")),
        name="diff_attn_meta" if meta_q else "diff_attn",
    )(*lams, sg, q, k, km, vt, vm, gate)


def _window_attn_kernel(sink_ref, q_ref, k_ref, ks_ref, km_ref, kms_ref, vt_ref, vm_ref, gate_ref, o_ref,
                        s_sc, mx_sc, p_sc, m_sc, *, nb, nq, blocks_per_step, meta_q):
    f32 = jnp.float32
    bf = jnp.bfloat16
    step = pl.program_id(1)
    n_keys = s_sc.shape[1]
    n_keys_padded = p_sc.shape[1]
    lane = lax.broadcasted_iota(jnp.int32, (1, LANES), 1)
    lo = lane < HEAD_DIM
    zero = jnp.zeros((LANES, LANES), bf)
    heads = (0, 2, 5, 7, 1, 3, 4, 6)

    km = km_ref[...]
    kms = kms_ref[...]
    _, vmt = _meta_kv(km_ref, vm_ref)
    key_i = lax.broadcasted_iota(jnp.int32, (LANES, LANES), 0)
    qry_i = lax.broadcasted_iota(jnp.int32, (LANES, LANES), 1)
    sink_row = jnp.concatenate([jnp.full((1, LANES), sink_ref[hd] * LOG2E, f32) for hd in heads], axis=1)
    for slot in range(2):
        p_sc[slot, n_keys:, :] = jnp.zeros((n_keys_padded - n_keys, p_sc.shape[2]), bf)

    def rows_of(ref, blk):
        return ref[pl.ds(pl.multiple_of(blk * BLOCK, BLOCK), BLOCK), :]

    def pieces(n):
        if meta_q:
            return ([k_ref[0:BLOCK, :], km], [ks_ref[0:BLOCK, :], kms], [vt_ref[0], vmt],
                    [(N_META + key_i - qry_i) <= WINDOW])
        prev = jnp.maximum(n - 1, 0)
        nxt = jnp.minimum(n + 1, nb - 1)
        return ([rows_of(k_ref, prev), rows_of(k_ref, n), rows_of(k_ref, nxt), km],
                [rows_of(ks_ref, prev), rows_of(ks_ref, n), rows_of(ks_ref, nxt), kms],
                [vt_ref[prev], vt_ref[n], vt_ref[nxt], vmt],
                [(key_i >= qry_i) & (n > 0), None, (key_i <= qry_i) & (n < nb - 1)])

    def scores(j):
        n = step * blocks_per_step + j
        q = _pad_rows(q_ref[j * nq:(j + 1) * nq, :], LANES)
        cols = [q[:, c * LANES:(c + 1) * LANES] for c in range(A_Q // LANES)]
        lo_of = [jnp.where(lo, c, zero) for c in cols]
        hi_of = [jnp.where(lo, zero, c) for c in cols]
        q_plain = jnp.concatenate([lo_of[0], lo_of[1], hi_of[2], hi_of[3]], axis=0)
        q_swap = jnp.concatenate([hi_of[0], hi_of[1], lo_of[2], lo_of[3]], axis=0)
        pk, pks, _, masks = pieces(n)
        s = jnp.concatenate(
            [lax.dot_general(jnp.concatenate(pk, axis=0), q_plain, _NT, preferred_element_type=f32),
             lax.dot_general(jnp.concatenate(pks, axis=0), q_swap, _NT, preferred_element_type=f32)],
            axis=1)
        cols = []
        for t in range(len(heads)):
            sh = s[:, t * LANES:(t + 1) * LANES]
            rows = [sh[r * LANES:(r + 1) * LANES, :] if mask is None
                    else jnp.where(mask, sh[r * LANES:(r + 1) * LANES, :], NEG_INF)
                    for r, mask in enumerate(masks)]
            rows.append(sh[len(masks) * LANES:, :])
            cols.append(jnp.concatenate(rows, axis=0))
        s = jnp.concatenate(cols, axis=1)
        s_sc[j % 2] = s
        mx_sc[j % 2] = jnp.max(s, axis=0, keepdims=True)

    def softmax(j):
        m = jnp.maximum(mx_sc[j % 2], sink_row)
        p_sc[j % 2, :n_keys, :] = jnp.exp2(s_sc[j % 2] - m).astype(bf)
        m_sc[j % 2] = m

    def pv(j):
        n = step * blocks_per_step + j
        _, _, pvt, _ = pieces(n)
        vt = jnp.concatenate(pvt, axis=1)
        vt = jnp.concatenate([vt, jnp.ones((SUM_ROWS, vt.shape[1]), bf)], axis=0)
        ot = jnp.dot(vt, p_sc[j % 2], preferred_element_type=f32)
        l = ot[2 * HEAD_DIM:2 * HEAD_DIM + 1, :] + jnp.exp2(sink_row - m_sc[j % 2])
        ot = ot[:2 * HEAD_DIM, :] * (1.0 / l)
        out_heads = [None] * A_HEADS
        for t, hd in enumerate(heads):
            g = hd // (A_HEADS // A_KV_HEADS)
            out_heads[hd] = ot[g * HEAD_DIM:(g + 1) * HEAD_DIM, t * LANES:(t + 1) * LANES]
        o = jnp.concatenate(out_heads, axis=0).T[:nq, :]
        o_ref[j * nq:(j + 1) * nq, :] = (o * gate_ref[j * nq:(j + 1) * nq, :].astype(f32)).astype(o_ref.dtype)

    scores(0)
    for j in range(blocks_per_step):
        if j + 1 < blocks_per_step:
            scores(j + 1)
        if j > 0:
            pv(j - 1)
        softmax(j)
    pv(blocks_per_step - 1)


def _window_attn(sink, q, k, ks, km, kms, vt, vm, gate, *, bsz, seq, blocks_per_step, meta_q):
    nb = seq // BLOCK
    if meta_q:
        nq, bps, n_steps, n_real_keys = N_META, 1, 1, BLOCK
        k_spec = pl.BlockSpec((BLOCK, A_KV), lambda b, n: (b * nb, 0))
        vt_spec = pl.BlockSpec((1, A_KV, LANES), lambda b, n: (b * nb, 0, 0))
    else:
        bps = math.gcd(blocks_per_step, nb)
        nq, n_steps, n_real_keys = BLOCK, nb // bps, 3 * BLOCK
        k_spec = pl.BlockSpec((seq, A_KV), lambda b, n: (b, 0))
        vt_spec = pl.BlockSpec((nb, A_KV, LANES), lambda b, n: (b, 0, 0))
    q_index = lambda b, n: (b * n_steps + n, 0)
    meta_spec = pl.BlockSpec((N_META, A_KV), lambda b, n: (b, 0))
    return pl.pallas_call(
        functools.partial(_window_attn_kernel, nb=nb, nq=nq, blocks_per_step=bps, meta_q=meta_q),
        grid=(bsz, n_steps),
        in_specs=[
            pl.BlockSpec(memory_space=pltpu.SMEM),
            pl.BlockSpec((bps * nq, A_Q), q_index),
            k_spec, k_spec, meta_spec, meta_spec, vt_spec, meta_spec,
            pl.BlockSpec((bps * nq, A_WIDTH), q_index),
        ],
        out_specs=pl.BlockSpec((bps * nq, A_WIDTH), q_index),
        out_shape=jax.ShapeDtypeStruct((bsz * n_steps * bps * nq, A_WIDTH), jnp.bfloat16),
        scratch_shapes=[
            pltpu.VMEM((2, n_real_keys + N_META, A_HEADS * LANES), jnp.float32),
            pltpu.VMEM((2, 1, A_HEADS * LANES), jnp.float32),
            pltpu.VMEM((2, n_real_keys + LANES, A_HEADS * LANES), jnp.bfloat16),
            pltpu.VMEM((2, 1, A_HEADS * LANES), jnp.float32),
        ],
        compiler_params=_cparams(("parallel", "arbitrary")),
        name="window_attn_meta" if meta_q else "window_attn",
    )(sink, q, k, ks, km, kms, vt, vm, gate)


def _rope_tables(pos):
    inv_freq = 1.0 / (ROPE_THETA ** (jnp.arange(0, HEAD_DIM, 2, dtype=jnp.float32) / HEAD_DIM))
    ang = pos.astype(jnp.float32)[:, None] * inv_freq[None, :]
    ang = jnp.concatenate([ang, ang, ang, ang], axis=-1)
    sign = jnp.where((jnp.arange(LANES) % HEAD_DIM) < HEAD_DIM // 2, -1.0, 1.0).astype(jnp.float32)
    return jnp.cos(ang), jnp.sin(ang) * sign[None, :]


def _encode_both(x_prompt, x_sample, meta_tokens, w_in, w_out, pre_norm_g, post_norm_g, sink_logits,
                 lambda_q1, lambda_k1, lambda_q2, lambda_k2, subln_g, *, tm, q_block):
    d = x_prompt.shape[-1]
    depth = w_in.shape[0]
    assert d == D_MODEL and tm % LANES == 0 and q_block % LANES == 0
    groups = []
    for x in (x_prompt, x_sample):
        bsz, seq, _ = x.shape
        assert seq % tm == 0 and seq % q_block == 0
        groups.append(dict(
            bsz=bsz, seq=seq,
            h_real=x.reshape(bsz * seq, d),
            h_meta=jnp.broadcast_to(meta_tokens[None], (bsz, N_META, d)).reshape(bsz * N_META, d),
            tab_real=_rope_tables(N_META + jnp.arange(seq)),
            tab_meta=_rope_tables(jnp.tile(jnp.arange(N_META), bsz)),
        ))

    for l in range(depth):
        lambda_init = 0.8 - 0.6 * math.exp(-0.3 * l)
        w_i = w_in[l].astype(jnp.bfloat16)
        w_o = w_out[l].astype(jnp.bfloat16)
        pre_g = pre_norm_g[l][None, :]
        post_g = post_norm_g[l][None, :]
        lams = (lambda_q1[l][None, :], lambda_k1[l][None, :], lambda_q2[l][None, :], lambda_k2[l][None, :])
        sg = subln_g[l][:, None]
        for grp in groups:
            bsz, seq = grp["bsz"], grp["seq"]
            (aq, ak, aks, avt, ag, bq, bk, bvt, bg) = _in_proj(
                grp["h_real"], pre_g, w_i, *grp["tab_real"], tm, True)
            (maq, mak, maks, mav, mag, mbq, mbk, mbv, mbg) = _in_proj(
                grp["h_meta"], pre_g, w_i, *grp["tab_meta"], bsz * N_META, False)
            win = functools.partial(_window_attn, sink_logits[l], bsz=bsz, seq=seq,
                                    blocks_per_step=WINDOW_BLOCKS_PER_STEP)
            oa = win(aq, ak, aks, mak, maks, avt, mav, ag, meta_q=False)
            moa = win(maq, ak, aks, mak, maks, avt, mav, mag, meta_q=True)
            dif = functools.partial(_diff_attn, lams, sg, bsz=bsz, seq=seq, tm=tm, q_block=q_block,
                                    tiles_per_chunk=DIFF_TILES_PER_CHUNK, lambda_init=lambda_init)
            ob = dif(bq, bk, mbk, bvt, mbv, bg, meta_q=False)
            mob = dif(mbq, bk, mbk, bvt, mbv, mbg, meta_q=True)
            grp["h_real"] = _out_proj(oa, ob, w_o[:A_WIDTH], w_o[A_WIDTH:], post_g, grp["h_real"], tm)
            grp["h_meta"] = _out_proj(moa, mob, w_o[:A_WIDTH], w_o[A_WIDTH:], post_g, grp["h_meta"],
                                      bsz * N_META)

    return tuple(g["h_real"].reshape(g["bsz"], g["seq"], d) for g in groups)


def kernel(x_prompt, x_sample, meta_tokens, w_in, w_out, pre_norm_g, post_norm_g, sink_logits,
           lambda_q1, lambda_k1, lambda_q2, lambda_k2, subln_g):
    return _encode_both(x_prompt, x_sample, meta_tokens, w_in, w_out, pre_norm_g, post_norm_g, sink_logits,
                        lambda_q1, lambda_k1, lambda_q2, lambda_k2, subln_g,
                        tm=ROW_TILE, q_block=DIFF_Q_BLOCK)
```

```python
import functools
import math

import jax
import jax.numpy as jnp
from jax import lax
from jax.experimental import pallas as pl
from jax.experimental.pallas import tpu as pltpu

D_MODEL = 1024
N_META = 16
HEAD_DIM = 64
BLOCK = 128
WINDOW = 128
ROPE_THETA = 10000.0
EPS = 1e-6
NEG_INF = -1e30

A_WIDTH = 512
A_HEADS = 8
A_KV_HEADS = 2
B_WIDTH = 512
B_VDIM = 128
B_HEADS = 4
A_Q = A_HEADS * HEAD_DIM
A_KV = A_KV_HEADS * HEAD_DIM
B_QK = 2 * B_HEADS * HEAD_DIM
IN_WIDTH = A_Q + 2 * A_KV + A_WIDTH + 2 * B_QK + 2 * B_WIDTH

OFF_AQ = 0
OFF_AK = OFF_AQ + A_Q
OFF_AV = OFF_AK + A_KV
OFF_AG = OFF_AV + A_KV
OFF_BQ = OFF_AG + A_WIDTH
OFF_BK = OFF_BQ + B_QK
OFF_BV = OFF_BK + B_QK
OFF_BG = OFF_BV + B_WIDTH

LANES = 128
LOG2E = math.log2(math.e)
Q_SCALE = HEAD_DIM ** -0.5 * LOG2E

ROW_TILE = 512
DIFF_Q_BLOCK = 512
DIFF_TILES_PER_CHUNK = 1
SUM_ROWS = 16
WINDOW_BLOCKS_PER_STEP = 16
VMEM_LIMIT = 56 * 1024 * 1024

_NT = (((1,), (1,)), ((), ()))


def _cparams(sem):
    return pltpu.CompilerParams(dimension_semantics=sem, vmem_limit_bytes=VMEM_LIMIT)


def _project_rows(x, g_ref, w_ref, cos_ref, sin_ref,
                  aq_ref, ak_ref, aks_ref, av_ref, ag_ref, bq_ref, bk_ref, bv_ref, bg_ref, transposed_v):
    ms = jnp.mean(x * x, axis=-1, keepdims=True)
    hn = (x * lax.rsqrt(ms + EPS) * g_ref[...]).astype(jnp.bfloat16)
    cos = cos_ref[...]
    sin = sin_ref[...]
    lane = lax.broadcasted_iota(jnp.int32, (1, LANES), 1)
    first_half = (lane % HEAD_DIM) < (HEAD_DIM // 2)

    def proj(off, width):
        return jnp.dot(hn, w_ref[:, off:off + width], preferred_element_type=jnp.float32)

    def rope(p):
        fwd = pltpu.roll(p, LANES - HEAD_DIM // 2, 1)
        bwd = pltpu.roll(p, HEAD_DIM // 2, 1)
        return p * cos + jnp.where(first_half, fwd, bwd) * sin

    def silu(p):
        return p * (1.0 / (1.0 + jnp.exp(-p)))

    tm = x.shape[0]

    p = proj(OFF_AQ, A_Q)
    for j in range(A_Q // LANES):
        sl = slice(j * LANES, (j + 1) * LANES)
        aq_ref[:, sl] = (rope(p[:, sl]) * Q_SCALE).astype(jnp.bfloat16)

    p = proj(OFF_AK, 2 * A_KV)
    k = rope(p[:, :A_KV])
    ak_ref[...] = k.astype(jnp.bfloat16)
    aks_ref[...] = pltpu.roll(k, HEAD_DIM, 1).astype(jnp.bfloat16)
    v = p[:, A_KV:]
    if transposed_v:
        vt = v.T
        for j in range(tm // LANES):
            av_ref[j] = vt[:, j * LANES:(j + 1) * LANES].astype(jnp.bfloat16)
    else:
        av_ref[...] = v.astype(jnp.bfloat16)

    ag_ref[...] = silu(proj(OFF_AG, A_WIDTH)).astype(jnp.bfloat16)

    p = proj(OFF_BQ, B_QK)
    for j in range(B_QK // LANES):
        sl = slice(j * LANES, (j + 1) * LANES)
        bq_ref[:, sl] = (rope(p[:, sl]) * Q_SCALE).astype(jnp.bfloat16)

    p = proj(OFF_BK, B_QK)
    for j in range(B_QK // LANES):
        sl = slice(j * LANES, (j + 1) * LANES)
        bk_ref[:, sl] = rope(p[:, sl]).astype(jnp.bfloat16)

    p = proj(OFF_BV, B_WIDTH)
    if transposed_v:
        bv_ref[0] = p.T.astype(jnp.bfloat16)
    else:
        bv_ref[...] = p.astype(jnp.bfloat16)

    bg_ref[...] = silu(proj(OFF_BG, B_WIDTH)).astype(jnp.bfloat16)


def _in_proj_kernel(h_ref, g_ref, w_ref, cos_ref, sin_ref, *out_refs, transposed_v):
    _project_rows(h_ref[...], g_ref, w_ref, cos_ref, sin_ref, *out_refs, transposed_v)


def _mid_proj_kernel(oa_ref, ob_ref, wa_ref, wb_ref, post_g_ref, h_ref, g_ref, w_ref, cos_ref, sin_ref,
                     h_out_ref, *out_refs, transposed_v):
    o = jnp.dot(oa_ref[...], wa_ref[...], preferred_element_type=jnp.float32)
    o = o + jnp.dot(ob_ref[...], wb_ref[...], preferred_element_type=jnp.float32)
    ms = jnp.mean(o * o, axis=-1, keepdims=True)
    x = h_ref[...] + o * lax.rsqrt(ms + EPS) * post_g_ref[...]
    h_out_ref[...] = x
    _project_rows(x, g_ref, w_ref, cos_ref, sin_ref, *out_refs, transposed_v)


def _proj_call(h, pre_g, w_bf16, cos_tab, sin_tab, tm, transposed_v, prev=None):
    rows = h.shape[0]
    nt = rows // tm
    n_tab = cos_tab.shape[0] // tm
    bf = jnp.bfloat16

    def row_spec(width):
        return pl.BlockSpec((tm, width), lambda t: (t, 0))

    def row_shape(width):
        return jax.ShapeDtypeStruct((rows, width), bf)

    if transposed_v:
        av_shape = jax.ShapeDtypeStruct((rows // LANES, A_KV, LANES), bf)
        av_spec = pl.BlockSpec((tm // LANES, A_KV, LANES), lambda t: (t, 0, 0))
        bv_shape = jax.ShapeDtypeStruct((nt, B_WIDTH, tm), bf)
        bv_spec = pl.BlockSpec((1, B_WIDTH, tm), lambda t: (t, 0, 0))
    else:
        av_shape, av_spec = row_shape(A_KV), row_spec(A_KV)
        bv_shape, bv_spec = row_shape(B_WIDTH), row_spec(B_WIDTH)
    out_shape = (row_shape(A_Q), row_shape(A_KV), row_shape(A_KV), av_shape, row_shape(A_WIDTH),
                 row_shape(B_QK), row_shape(B_QK), bv_shape, row_shape(B_WIDTH))
    out_specs = (row_spec(A_Q), row_spec(A_KV), row_spec(A_KV), av_spec, row_spec(A_WIDTH),
                 row_spec(B_QK), row_spec(B_QK), bv_spec, row_spec(B_WIDTH))
    tab_spec = pl.BlockSpec((tm, LANES), lambda t: (t % n_tab, 0))
    h_spec = pl.BlockSpec((tm, D_MODEL), lambda t: (t, 0))
    gain_spec = pl.BlockSpec((1, D_MODEL), lambda t: (0, 0))
    in_specs = [h_spec, gain_spec, pl.BlockSpec((D_MODEL, IN_WIDTH), lambda t: (0, 0)), tab_spec, tab_spec]
    args = (h, pre_g, w_bf16, cos_tab, sin_tab)
    if prev is None:
        kernel, name = _in_proj_kernel, "in_proj"
    else:
        kernel, name = _mid_proj_kernel, "mid_proj"
        oa, ob, wa, wb, post_g = prev
        in_specs = [row_spec(A_WIDTH), row_spec(B_WIDTH),
                    pl.BlockSpec((A_WIDTH, D_MODEL), lambda t: (0, 0)),
                    pl.BlockSpec((B_WIDTH, D_MODEL), lambda t: (0, 0)), gain_spec] + in_specs
        args = (oa, ob, wa, wb, post_g) + args
        out_shape = (jax.ShapeDtypeStruct(h.shape, h.dtype),) + out_shape
        out_specs = (h_spec,) + out_specs
    return pl.pallas_call(
        functools.partial(kernel, transposed_v=transposed_v),
        grid=(nt,), in_specs=in_specs, out_specs=out_specs, out_shape=out_shape,
        compiler_params=_cparams(("parallel",)), name=name if transposed_v else name + "_meta",
    )(*args)


def _out_proj_kernel(oa_ref, ob_ref, wa_ref, wb_ref, g_ref, h_ref, out_ref):
    o = jnp.dot(oa_ref[...], wa_ref[...], preferred_element_type=jnp.float32)
    o = o + jnp.dot(ob_ref[...], wb_ref[...], preferred_element_type=jnp.float32)
    ms = jnp.mean(o * o, axis=-1, keepdims=True)
    out_ref[...] = h_ref[...] + o * lax.rsqrt(ms + EPS) * g_ref[...]


def _out_proj(oa, ob, wa, wb, post_g, h, tm):
    rows = h.shape[0]
    return pl.pallas_call(
        _out_proj_kernel,
        grid=(rows // tm,),
        in_specs=[
            pl.BlockSpec((tm, A_WIDTH), lambda t: (t, 0)),
            pl.BlockSpec((tm, B_WIDTH), lambda t: (t, 0)),
            pl.BlockSpec((A_WIDTH, D_MODEL), lambda t: (0, 0)),
            pl.BlockSpec((B_WIDTH, D_MODEL), lambda t: (0, 0)),
            pl.BlockSpec((1, D_MODEL), lambda t: (0, 0)),
            pl.BlockSpec((tm, D_MODEL), lambda t: (t, 0)),
        ],
        out_specs=pl.BlockSpec((tm, D_MODEL), lambda t: (t, 0)),
        out_shape=jax.ShapeDtypeStruct(h.shape, h.dtype),
        compiler_params=_cparams(("parallel",)), name="out_proj",
    )(oa, ob, wa, wb, post_g, h)


def _pad_rows(x, rows):
    if x.shape[0] == rows:
        return x
    return jnp.concatenate([x, jnp.zeros((rows - x.shape[0], x.shape[1]), x.dtype)], axis=0)


def _meta_kv(km_ref, vm_ref):
    kmp = _pad_rows(km_ref[...], LANES)
    vmt = _pad_rows(vm_ref[...].astype(jnp.float32), LANES).T.astype(jnp.bfloat16)
    return kmp, vmt


def _diff_attn_kernel(lq1_ref, lk1_ref, lq2_ref, lk2_ref, sg_ref,
                      q_ref, k_ref, km_ref, vt_ref, vm_ref, gate_ref, o_ref,
                      qcat_sc, m_sc, acc_sc, s_sc, mx_sc, sm_sc, mxm_sc, p_sc, alpha_sc,
                      *, n_chunks, tiles_per_chunk, n_units, nq, n_qblocks, lambda_init):
    f32 = jnp.float32
    bq = n_units * nq
    ck = tiles_per_chunk * vt_ref.shape[2]
    lam = (jnp.exp(jnp.sum(lq1_ref[...] * lk1_ref[...], axis=-1, keepdims=True))
           - jnp.exp(jnp.sum(lq2_ref[...] * lk2_ref[...], axis=-1, keepdims=True)) + lambda_init)
    sg = sg_ref[...]
    lane = lax.broadcasted_iota(jnp.int32, (1, LANES), 1)
    comp0 = lane < HEAD_DIM
    kmp, vmt = _meta_kv(km_ref, vm_ref)
    meta_valid = lax.broadcasted_iota(jnp.int32, (LANES, 1), 0) < N_META

    def with_ones(vt):
        return jnp.concatenate([vt, jnp.ones((SUM_ROWS, vt.shape[1]), vt.dtype)], axis=0)

    def block_rows(ref, qb):
        return ref[pl.ds(pl.multiple_of(qb * bq, bq), bq), :]

    def start_block(qb):
        qrows = block_rows(q_ref, qb)
        qcats = []
        for u in range(n_units):
            q = _pad_rows(qrows[u * nq:(u + 1) * nq, :], LANES)
            zero = jnp.zeros_like(q)
            qcats += [jnp.where(comp0, q, zero), jnp.where(comp0, zero, q)]
        qcat = jnp.concatenate(qcats, axis=0)
        qcat_sc[...] = qcat
        s = lax.dot_general(kmp, qcat, _NT, preferred_element_type=f32)
        s = jnp.where(meta_valid, s, NEG_INF)
        sm_sc[...] = s
        mxm_sc[...] = jnp.max(s, axis=0, keepdims=True)
        scores(0)

    def scores(c):
        s = lax.dot_general(k_ref[c * ck:(c + 1) * ck, :], qcat_sc[...], _NT,
                            preferred_element_type=f32)
        s_sc[c % 2] = s
        mx_sc[c % 2] = jnp.max(s, axis=0, keepdims=True)

    def softmax(c):
        m_old = m_sc[...]
        m_new = jnp.maximum(m_old, mx_sc[c % 2])
        alpha = jnp.exp2(m_old - m_new)
        p_sc[c % 2] = jnp.exp2(s_sc[c % 2] - m_new).astype(jnp.bfloat16)
        alpha_sc[c % 2] = alpha
        m_sc[...] = m_new

    def pv(c):
        vt = jnp.concatenate([vt_ref[c * tiles_per_chunk + i] for i in range(tiles_per_chunk)], axis=1)
        acc_sc[...] = alpha_sc[c % 2] * acc_sc[...] + jnp.dot(with_ones(vt), p_sc[c % 2],
                                                               preferred_element_type=f32)

    def finish_block(qb):
        a_all = acc_sc[:B_VDIM, :] * (1.0 / acc_sc[B_VDIM:B_VDIM + 1, :])
        gate = block_rows(gate_ref, qb)
        outs = []
        for u in range(n_units):
            a = a_all[:, u * 2 * LANES:(u + 1) * 2 * LANES]
            o = a[:, :LANES] - lam * a[:, LANES:]
            ms = jnp.mean(o * o, axis=0, keepdims=True)
            y = (o * lax.rsqrt(ms + EPS) * sg) * (1.0 - lambda_init)
            yt = y.T[:nq, :]
            outs.append((yt * gate[u * nq:(u + 1) * nq, :].astype(f32)).astype(o_ref.dtype))
        o_ref[pl.ds(pl.multiple_of(qb * bq, bq), bq), :] = jnp.concatenate(outs, axis=0)

    def block(qb, carry):
        m = mxm_sc[...]
        p = jnp.exp2(sm_sc[...] - m)
        m_sc[...] = m
        acc_sc[...] = jnp.dot(with_ones(vmt), p.astype(jnp.bfloat16), preferred_element_type=f32)
        for c in range(n_chunks):
            softmax(c)
            if c + 1 < n_chunks:
                scores(c + 1)
            if c > 0:
                pv(c - 1)
        pv(n_chunks - 1)
        start_block(jnp.minimum(qb + 1, n_qblocks - 1))
        finish_block(qb)
        return carry

    start_block(0)
    lax.fori_loop(0, n_qblocks, block, 0)


def _diff_attn(lams, sg, q, k, km, vt, vm, gate, *, bsz, seq, tm, q_block, tiles_per_chunk, meta_q,
               lambda_init):
    n_tiles = seq // tm
    tiles_per_chunk = math.gcd(tiles_per_chunk, n_tiles)
    ck = tiles_per_chunk * tm
    n_chunks = n_tiles // tiles_per_chunk
    if meta_q:
        nq, n_units, n_qblocks = N_META, 1, 1
    else:
        nq, n_units, n_qblocks = LANES, q_block // LANES, seq // q_block
    q_rows = nq * n_units * n_qblocks
    width = n_units * 2 * LANES
    kernel = functools.partial(_diff_attn_kernel, n_chunks=n_chunks, tiles_per_chunk=tiles_per_chunk,
                               n_units=n_units, nq=nq, n_qblocks=n_qblocks, lambda_init=lambda_init)
    lam_spec = pl.BlockSpec((1, HEAD_DIM), lambda b, h: (0, 0))
    bh = lambda b, h: (b, h)
    return pl.pallas_call(
        kernel,
        grid=(bsz, B_HEADS),
        in_specs=[
            lam_spec, lam_spec, lam_spec, lam_spec,
            pl.BlockSpec((B_VDIM, 1), lambda b, h: (0, 0)),
            pl.BlockSpec((q_rows, LANES), bh),
            pl.BlockSpec((seq, LANES), bh),
            pl.BlockSpec((N_META, LANES), bh),
            pl.BlockSpec((n_tiles, B_VDIM, tm), lambda b, h: (b, h, 0)),
            pl.BlockSpec((N_META, B_VDIM), bh),
            pl.BlockSpec((q_rows, B_VDIM), bh),
        ],
        out_specs=pl.BlockSpec((q_rows, B_VDIM), bh),
        out_shape=jax.ShapeDtypeStruct((bsz * q_rows, B_WIDTH), jnp.bfloat16),
        scratch_shapes=[
            pltpu.VMEM((width, LANES), jnp.bfloat16),
            pltpu.VMEM((1, width), jnp.float32),
            pltpu.VMEM((B_VDIM + SUM_ROWS, width), jnp.float32),
            pltpu.VMEM((2, ck, width), jnp.float32),
            pltpu.VMEM((2, 1, width), jnp.float32),
            pltpu.VMEM((LANES, width), jnp.float32),
            pltpu.VMEM((1, width), jnp.float32),
            pltpu.VMEM((2, ck, width), jnp.bfloat16),
            pltpu.VMEM((2, 1, width), jnp.float32),
        ],
        compiler_params=_cparams(("parallel", "arbitrary")),
        name="diff_attn_meta" if meta_q else "diff_attn",
    )(*lams, sg, q, k, km, vt, vm, gate)


def _window_attn_kernel(sink_ref, q_ref, k_ref, ks_ref, km_ref, kms_ref, vt_ref, vm_ref, gate_ref, o_ref,
                        s_sc, mx_sc, p_sc, m_sc, *, nb, nq, blocks_per_step, meta_q):
    f32 = jnp.float32
    bf = jnp.bfloat16
    step = pl.program_id(1)
    n_keys = s_sc.shape[1]
    n_keys_padded = p_sc.shape[1]
    lane = lax.broadcasted_iota(jnp.int32, (1, LANES), 1)
    lo = lane < HEAD_DIM
    zero = jnp.zeros((LANES, LANES), bf)
    heads = (0, 2, 5, 7, 1, 3, 4, 6)

    km = km_ref[...]
    kms = kms_ref[...]
    _, vmt = _meta_kv(km_ref, vm_ref)
    key_i = lax.broadcasted_iota(jnp.int32, (LANES, LANES), 0)
    qry_i = lax.broadcasted_iota(jnp.int32, (LANES, LANES), 1)
    sink_row = jnp.concatenate([jnp.full((1, LANES), sink_ref[hd] * LOG2E, f32) for hd in heads], axis=1)
    for slot in range(2):
        p_sc[slot, n_keys:, :] = jnp.zeros((n_keys_padded - n_keys, p_sc.shape[2]), bf)

    def rows_of(ref, blk):
        return ref[pl.ds(pl.multiple_of(blk * BLOCK, BLOCK), BLOCK), :]

    def pieces(n):
        if meta_q:
            return ([k_ref[0:BLOCK, :], km], [ks_ref[0:BLOCK, :], kms], [vt_ref[0], vmt],
                    [(N_META + key_i - qry_i) <= WINDOW])
        prev = jnp.maximum(n - 1, 0)
        nxt = jnp.minimum(n + 1, nb - 1)
        return ([rows_of(k_ref, prev), rows_of(k_ref, n), rows_of(k_ref, nxt), km],
                [rows_of(ks_ref, prev), rows_of(ks_ref, n), rows_of(ks_ref, nxt), kms],
                [vt_ref[prev], vt_ref[n], vt_ref[nxt], vmt],
                [(key_i >= qry_i) & (n > 0), None, (key_i <= qry_i) & (n < nb - 1)])

    def scores(j):
        n = step * blocks_per_step + j
        q = _pad_rows(q_ref[j * nq:(j + 1) * nq, :], LANES)
        cols = [q[:, c * LANES:(c + 1) * LANES] for c in range(A_Q // LANES)]
        lo_of = [jnp.where(lo, c, zero) for c in cols]
        hi_of = [jnp.where(lo, zero, c) for c in cols]
        q_plain = jnp.concatenate([lo_of[0], lo_of[1], hi_of[2], hi_of[3]], axis=0)
        q_swap = jnp.concatenate([hi_of[0], hi_of[1], lo_of[2], lo_of[3]], axis=0)
        pk, pks, _, masks = pieces(n)
        s = jnp.concatenate(
            [lax.dot_general(jnp.concatenate(pk, axis=0), q_plain, _NT, preferred_element_type=f32),
             lax.dot_general(jnp.concatenate(pks, axis=0), q_swap, _NT, preferred_element_type=f32)],
            axis=1)
        cols = []
        for t in range(len(heads)):
            sh = s[:, t * LANES:(t + 1) * LANES]
            rows = [sh[r * LANES:(r + 1) * LANES, :] if mask is None
                    else jnp.where(mask, sh[r * LANES:(r + 1) * LANES, :], NEG_INF)
                    for r, mask in enumerate(masks)]
            rows.append(sh[len(masks) * LANES:, :])
            cols.append(jnp.concatenate(rows, axis=0))
        s = jnp.concatenate(cols, axis=1)
        s_sc[j % 2] = s
        mx_sc[j % 2] = jnp.max(s, axis=0, keepdims=True)

    def softmax(j):
        m = jnp.maximum(mx_sc[j % 2], sink_row)
        p_sc[j % 2, :n_keys, :] = jnp.exp2(s_sc[j % 2] - m).astype(bf)
        m_sc[j % 2] = m

    def pv(j):
        n = step * blocks_per_step + j
        _, _, pvt, _ = pieces(n)
        vt = jnp.concatenate(pvt, axis=1)
        vt = jnp.concatenate([vt, jnp.ones((SUM_ROWS, vt.shape[1]), bf)], axis=0)
        ot = jnp.dot(vt, p_sc[j % 2], preferred_element_type=f32)
        l = ot[2 * HEAD_DIM:2 * HEAD_DIM + 1, :] + jnp.exp2(sink_row - m_sc[j % 2])
        ot = ot[:2 * HEAD_DIM, :] * (1.0 / l)
        out_heads = [None] * A_HEADS
        for t, hd in enumerate(heads):
            g = hd // (A_HEADS // A_KV_HEADS)
            out_heads[hd] = ot[g * HEAD_DIM:(g + 1) * HEAD_DIM, t * LANES:(t + 1) * LANES]
        o = jnp.concatenate(out_heads, axis=0).T[:nq, :]
        o_ref[j * nq:(j + 1) * nq, :] = (o * gate_ref[j * nq:(j + 1) * nq, :].astype(f32)).astype(o_ref.dtype)

    scores(0)
    for j in range(blocks_per_step):
        if j + 1 < blocks_per_step:
            scores(j + 1)
        if j > 0:
            pv(j - 1)
        softmax(j)
    pv(blocks_per_step - 1)


def _window_attn(sink, q, k, ks, km, kms, vt, vm, gate, *, bsz, seq, blocks_per_step, meta_q):
    nb = seq // BLOCK
    if meta_q:
        nq, bps, n_steps, n_real_keys = N_META, 1, 1, BLOCK
        k_spec = pl.BlockSpec((BLOCK, A_KV), lambda b, n: (b * nb, 0))
        vt_spec = pl.BlockSpec((1, A_KV, LANES), lambda b, n: (b * nb, 0, 0))
    else:
        bps = math.gcd(blocks_per_step, nb)
        nq, n_steps, n_real_keys = BLOCK, nb // bps, 3 * BLOCK
        k_spec = pl.BlockSpec((seq, A_KV), lambda b, n: (b, 0))
        vt_spec = pl.BlockSpec((nb, A_KV, LANES), lambda b, n: (b, 0, 0))
    q_index = lambda b, n: (b * n_steps + n, 0)
    meta_spec = pl.BlockSpec((N_META, A_KV), lambda b, n: (b, 0))
    return pl.pallas_call(
        functools.partial(_window_attn_kernel, nb=nb, nq=nq, blocks_per_step=bps, meta_q=meta_q),
        grid=(bsz, n_steps),
        in_specs=[
            pl.BlockSpec(memory_space=pltpu.SMEM),
            pl.BlockSpec((bps * nq, A_Q), q_index),
            k_spec, k_spec, meta_spec, meta_spec, vt_spec, meta_spec,
            pl.BlockSpec((bps * nq, A_WIDTH), q_index),
        ],
        out_specs=pl.BlockSpec((bps * nq, A_WIDTH), q_index),
        out_shape=jax.ShapeDtypeStruct((bsz * n_steps * bps * nq, A_WIDTH), jnp.bfloat16),
        scratch_shapes=[
            pltpu.VMEM((2, n_real_keys + N_META, A_HEADS * LANES), jnp.float32),
            pltpu.VMEM((2, 1, A_HEADS * LANES), jnp.float32),
            pltpu.VMEM((2, n_real_keys + LANES, A_HEADS * LANES), jnp.bfloat16),
            pltpu.VMEM((2, 1, A_HEADS * LANES), jnp.float32),
        ],
        compiler_params=_cparams(("parallel", "arbitrary")),
        name="window_attn_meta" if meta_q else "window_attn",
    )(sink, q, k, ks, km, kms, vt, vm, gate)


def _rope_tables(pos):
    inv_freq = 1.0 / (ROPE_THETA ** (jnp.arange(0, HEAD_DIM, 2, dtype=jnp.float32) / HEAD_DIM))
    ang = pos.astype(jnp.float32)[:, None] * inv_freq[None, :]
    ang = jnp.concatenate([ang, ang, ang, ang], axis=-1)
    sign = jnp.where((jnp.arange(LANES) % HEAD_DIM) < HEAD_DIM // 2, -1.0, 1.0).astype(jnp.float32)
    return jnp.cos(ang), jnp.sin(ang) * sign[None, :]


def _encode_both(x_prompt, x_sample, meta_tokens, w_in, w_out, pre_norm_g, post_norm_g, sink_logits,
                 lambda_q1, lambda_k1, lambda_q2, lambda_k2, subln_g, *, tm, q_block):
    d = x_prompt.shape[-1]
    depth = w_in.shape[0]
    assert d == D_MODEL and tm % LANES == 0 and q_block % LANES == 0
    groups = []
    for x in (x_prompt, x_sample):
        bsz, seq, _ = x.shape
        assert seq % tm == 0 and seq % q_block == 0
        groups.append(dict(
            bsz=bsz, seq=seq,
            h_real=x.reshape(bsz * seq, d),
            h_meta=jnp.broadcast_to(meta_tokens[None], (bsz, N_META, d)).reshape(bsz * N_META, d),
            tab_real=_rope_tables(N_META + jnp.arange(seq)),
            tab_meta=_rope_tables(jnp.tile(jnp.arange(N_META), bsz)),
        ))

    for l in range(depth):
        lambda_init = 0.8 - 0.6 * math.exp(-0.3 * l)
        w_i = w_in[l].astype(jnp.bfloat16)
        w_o = w_out[l].astype(jnp.bfloat16)
        pre_g = pre_norm_g[l][None, :]
        post_g = post_norm_g[l][None, :]
        lams = (lambda_q1[l][None, :], lambda_k1[l][None, :], lambda_q2[l][None, :], lambda_k2[l][None, :])
        sg = subln_g[l][:, None]
        for grp in groups:
            bsz, seq = grp["bsz"], grp["seq"]
            real = _proj_call(grp["h_real"], pre_g, w_i, *grp["tab_real"], tm, True, grp.get("prev_real"))
            meta = _proj_call(grp["h_meta"], pre_g, w_i, *grp["tab_meta"], bsz * N_META, False,
                              grp.get("prev_meta"))
            if l > 0:
                grp["h_real"], grp["h_meta"] = real[0], meta[0]
                real, meta = real[1:], meta[1:]
            (aq, ak, aks, avt, ag, bq, bk, bvt, bg) = real
            (maq, mak, maks, mav, mag, mbq, mbk, mbv, mbg) = meta
            win = functools.partial(_window_attn, sink_logits[l], bsz=bsz, seq=seq,
                                    blocks_per_step=WINDOW_BLOCKS_PER_STEP)
            oa = win(aq, ak, aks, mak, maks, avt, mav, ag, meta_q=False)
            moa = win(maq, ak, aks, mak, maks, avt, mav, mag, meta_q=True)
            dif = functools.partial(_diff_attn, lams, sg, bsz=bsz, seq=seq, tm=tm, q_block=q_block,
                                    tiles_per_chunk=DIFF_TILES_PER_CHUNK, lambda_init=lambda_init)
            ob = dif(bq, bk, mbk, bvt, mbv, bg, meta_q=False)
            mob = dif(mbq, bk, mbk, bvt, mbv, mbg, meta_q=True)
            grp["prev_real"] = (oa, ob, w_o[:A_WIDTH], w_o[A_WIDTH:], post_g)
            grp["prev_meta"] = (moa, mob, w_o[:A_WIDTH], w_o[A_WIDTH:], post_g)

    return tuple(_out_proj(*g["prev_real"], g["h_real"], tm).reshape(g["bsz"], g["seq"], d) for g in groups)


def kernel(x_prompt, x_sample, meta_tokens, w_in, w_out, pre_norm_g, post_norm_g, sink_logits,
           lambda_q1, lambda_k1, lambda_q2, lambda_k2, subln_g):
    return _encode_both(x_prompt, x_sample, meta_tokens, w_in, w_out, pre_norm_g, post_norm_g, sink_logits,
                        lambda_q1, lambda_k1, lambda_q2, lambda_k2, subln_g,
                        tm=ROW_TILE, q_block=DIFF_Q_BLOCK)
```

```python
import functools
import math

import jax
import jax.numpy as jnp
from jax import lax
from jax.experimental import pallas as pl
from jax.experimental.pallas import tpu as pltpu

D_MODEL = 1024
N_META = 16
HEAD_DIM = 64
BLOCK = 128
WINDOW = 128
ROPE_THETA = 10000.0
EPS = 1e-6
NEG_INF = -1e30

A_WIDTH = 512
A_HEADS = 8
A_KV_HEADS = 2
B_WIDTH = 512
B_VDIM = 128
B_HEADS = 4
A_Q = A_HEADS * HEAD_DIM
A_KV = A_KV_HEADS * HEAD_DIM
B_QK = 2 * B_HEADS * HEAD_DIM
IN_WIDTH = A_Q + 2 * A_KV + A_WIDTH + 2 * B_QK + 2 * B_WIDTH

OFF_AQ = 0
OFF_AK = OFF_AQ + A_Q
OFF_AV = OFF_AK + A_KV
OFF_AG = OFF_AV + A_KV
OFF_BQ = OFF_AG + A_WIDTH
OFF_BK = OFF_BQ + B_QK
OFF_BV = OFF_BK + B_QK
OFF_BG = OFF_BV + B_WIDTH

LANES = 128
LOG2E = math.log2(math.e)
Q_SCALE = HEAD_DIM ** -0.5 * LOG2E

ROW_TILE = 512
DIFF_Q_BLOCK = 512
DIFF_TILES_PER_CHUNK = 1
SUM_ROWS = 16
WINDOW_BLOCKS_PER_STEP = 16
VMEM_LIMIT = 56 * 1024 * 1024

_NT = (((1,), (1,)), ((), ()))


def _cparams(sem):
    return pltpu.CompilerParams(dimension_semantics=sem, vmem_limit_bytes=VMEM_LIMIT)


def _project_rows(x, g_ref, w_ref, cos_ref, sin_ref,
                  aq_ref, ak_ref, aks_ref, av_ref, ag_ref, bq_ref, bk_ref, bv_ref, bg_ref, transposed_v):
    ms = jnp.mean(x * x, axis=-1, keepdims=True)
    hn = (x * lax.rsqrt(ms + EPS) * g_ref[...]).astype(jnp.bfloat16)
    cos = cos_ref[...]
    sin = sin_ref[...]
    lane = lax.broadcasted_iota(jnp.int32, (1, LANES), 1)
    first_half = (lane % HEAD_DIM) < (HEAD_DIM // 2)

    def proj(off, width):
        return jnp.dot(hn, w_ref[:, off:off + width], preferred_element_type=jnp.float32)

    def rope(p):
        fwd = pltpu.roll(p, LANES - HEAD_DIM // 2, 1)
        bwd = pltpu.roll(p, HEAD_DIM // 2, 1)
        return p * cos + jnp.where(first_half, fwd, bwd) * sin

    def silu(p):
        return p * (1.0 / (1.0 + jnp.exp(-p)))

    tm = x.shape[0]

    p = proj(OFF_AQ, A_Q)
    for j in range(A_Q // LANES):
        sl = slice(j * LANES, (j + 1) * LANES)
        aq_ref[:, sl] = (rope(p[:, sl]) * Q_SCALE).astype(jnp.bfloat16)

    p = proj(OFF_AK, 2 * A_KV)
    k = rope(p[:, :A_KV])
    ak_ref[...] = k.astype(jnp.bfloat16)
    aks_ref[...] = pltpu.roll(k, HEAD_DIM, 1).astype(jnp.bfloat16)
    v = p[:, A_KV:]
    if transposed_v:
        vt = v.T
        for j in range(tm // LANES):
            av_ref[j] = vt[:, j * LANES:(j + 1) * LANES].astype(jnp.bfloat16)
    else:
        av_ref[...] = v.astype(jnp.bfloat16)

    ag_ref[...] = silu(proj(OFF_AG, A_WIDTH)).astype(jnp.bfloat16)

    p = proj(OFF_BQ, B_QK)
    for j in range(B_QK // LANES):
        sl = slice(j * LANES, (j + 1) * LANES)
        bq_ref[:, sl] = (rope(p[:, sl]) * Q_SCALE).astype(jnp.bfloat16)

    p = proj(OFF_BK, B_QK)
    for j in range(B_QK // LANES):
        sl = slice(j * LANES, (j + 1) * LANES)
        bk_ref[:, sl] = rope(p[:, sl]).astype(jnp.bfloat16)

    p = proj(OFF_BV, B_WIDTH)
    if transposed_v:
        bv_ref[0] = p.T.astype(jnp.bfloat16)
    else:
        bv_ref[...] = p.astype(jnp.bfloat16)

    bg_ref[...] = silu(proj(OFF_BG, B_WIDTH)).astype(jnp.bfloat16)


def _in_proj_kernel(h_ref, g_ref, w_ref, cos_ref, sin_ref, *out_refs, transposed_v):
    _project_rows(h_ref[...], g_ref, w_ref, cos_ref, sin_ref, *out_refs, transposed_v)


def _mid_proj_kernel(oa_ref, ob_ref, wa_ref, wb_ref, post_g_ref, h_ref, g_ref, w_ref, cos_ref, sin_ref,
                     h_out_ref, *out_refs, transposed_v):
    o = jnp.dot(oa_ref[...], wa_ref[...], preferred_element_type=jnp.float32)
    o = o + jnp.dot(ob_ref[...], wb_ref[...], preferred_element_type=jnp.float32)
    ms = jnp.mean(o * o, axis=-1, keepdims=True)
    x = h_ref[...] + o * lax.rsqrt(ms + EPS) * post_g_ref[...]
    h_out_ref[...] = x
    _project_rows(x, g_ref, w_ref, cos_ref, sin_ref, *out_refs, transposed_v)


def _proj_call(h, pre_g, w_bf16, cos_tab, sin_tab, tm, transposed_v, prev=None):
    rows = h.shape[0]
    nt = rows // tm
    n_tab = cos_tab.shape[0] // tm
    bf = jnp.bfloat16

    def row_spec(width):
        return pl.BlockSpec((tm, width), lambda t: (t, 0))

    def row_shape(width):
        return jax.ShapeDtypeStruct((rows, width), bf)

    if transposed_v:
        av_shape = jax.ShapeDtypeStruct((rows // LANES, A_KV, LANES), bf)
        av_spec = pl.BlockSpec((tm // LANES, A_KV, LANES), lambda t: (t, 0, 0))
        bv_shape = jax.ShapeDtypeStruct((nt, B_WIDTH, tm), bf)
        bv_spec = pl.BlockSpec((1, B_WIDTH, tm), lambda t: (t, 0, 0))
    else:
        av_shape, av_spec = row_shape(A_KV), row_spec(A_KV)
        bv_shape, bv_spec = row_shape(B_WIDTH), row_spec(B_WIDTH)
    out_shape = (row_shape(A_Q), row_shape(A_KV), row_shape(A_KV), av_shape, row_shape(A_WIDTH),
                 row_shape(B_QK), row_shape(B_QK), bv_shape, row_shape(B_WIDTH))
    out_specs = (row_spec(A_Q), row_spec(A_KV), row_spec(A_KV), av_spec, row_spec(A_WIDTH),
                 row_spec(B_QK), row_spec(B_QK), bv_spec, row_spec(B_WIDTH))
    tab_spec = pl.BlockSpec((tm, LANES), lambda t: (t % n_tab, 0))
    h_spec = pl.BlockSpec((tm, D_MODEL), lambda t: (t, 0))
    gain_spec = pl.BlockSpec((1, D_MODEL), lambda t: (0, 0))
    in_specs = [h_spec, gain_spec, pl.BlockSpec((D_MODEL, IN_WIDTH), lambda t: (0, 0)), tab_spec, tab_spec]
    args = (h, pre_g, w_bf16, cos_tab, sin_tab)
    if prev is None:
        kernel, name = _in_proj_kernel, "in_proj"
    else:
        kernel, name = _mid_proj_kernel, "mid_proj"
        oa, ob, wa, wb, post_g = prev
        in_specs = [row_spec(A_WIDTH), row_spec(B_WIDTH),
                    pl.BlockSpec((A_WIDTH, D_MODEL), lambda t: (0, 0)),
                    pl.BlockSpec((B_WIDTH, D_MODEL), lambda t: (0, 0)), gain_spec] + in_specs
        args = (oa, ob, wa, wb, post_g) + args
        out_shape = (jax.ShapeDtypeStruct(h.shape, h.dtype),) + out_shape
        out_specs = (h_spec,) + out_specs
    return pl.pallas_call(
        functools.partial(kernel, transposed_v=transposed_v),
        grid=(nt,), in_specs=in_specs, out_specs=out_specs, out_shape=out_shape,
        compiler_params=_cparams(("parallel",)), name=name if transposed_v else name + "_meta",
    )(*args)


def _out_proj_kernel(oa_ref, ob_ref, wa_ref, wb_ref, g_ref, h_ref, out_ref):
    o = jnp.dot(oa_ref[...], wa_ref[...], preferred_element_type=jnp.float32)
    o = o + jnp.dot(ob_ref[...], wb_ref[...], preferred_element_type=jnp.float32)
    ms = jnp.mean(o * o, axis=-1, keepdims=True)
    out_ref[...] = h_ref[...] + o * lax.rsqrt(ms + EPS) * g_ref[...]


def _out_proj(oa, ob, wa, wb, post_g, h, tm):
    rows = h.shape[0]
    return pl.pallas_call(
        _out_proj_kernel,
        grid=(rows // tm,),
        in_specs=[
            pl.BlockSpec((tm, A_WIDTH), lambda t: (t, 0)),
            pl.BlockSpec((tm, B_WIDTH), lambda t: (t, 0)),
            pl.BlockSpec((A_WIDTH, D_MODEL), lambda t: (0, 0)),
            pl.BlockSpec((B_WIDTH, D_MODEL), lambda t: (0, 0)),
            pl.BlockSpec((1, D_MODEL), lambda t: (0, 0)),
            pl.BlockSpec((tm, D_MODEL), lambda t: (t, 0)),
        ],
        out_specs=pl.BlockSpec((tm, D_MODEL), lambda t: (t, 0)),
        out_shape=jax.ShapeDtypeStruct(h.shape, h.dtype),
        compiler_params=_cparams(("parallel",)), name="out_proj",
    )(oa, ob, wa, wb, post_g, h)


def _pad_rows(x, rows):
    if x.shape[0] == rows:
        return x
    return jnp.concatenate([x, jnp.zeros((rows - x.shape[0], x.shape[1]), x.dtype)], axis=0)


def _meta_kv(km_ref, vm_ref):
    kmp = _pad_rows(km_ref[...], LANES)
    vmt = _pad_rows(vm_ref[...].astype(jnp.float32), LANES).T.astype(jnp.bfloat16)
    return kmp, vmt


def _diff_attn_kernel(lq1_ref, lk1_ref, lq2_ref, lk2_ref, sg_ref,
                      q_ref, k_ref, km_ref, vt_ref, vm_ref, gate_ref, o_ref,
                      qcat_sc, m_sc, acc_sc, s_sc, mx_sc, sm_sc, mxm_sc, p_sc, alpha_sc,
                      *, n_chunks, tiles_per_chunk, n_units, nq, n_qblocks, n_heads, lambda_init):
    f32 = jnp.float32
    bq = n_units * nq
    ck = tiles_per_chunk * vt_ref.shape[2]
    lam = (jnp.exp(jnp.sum(lq1_ref[...] * lk1_ref[...], axis=-1, keepdims=True))
           - jnp.exp(jnp.sum(lq2_ref[...] * lk2_ref[...], axis=-1, keepdims=True)) + lambda_init)
    sg = sg_ref[...]
    lane = lax.broadcasted_iota(jnp.int32, (1, LANES), 1)
    comp0 = lane < HEAD_DIM
    heads = range(n_heads)

    def cols(hd):
        return slice(hd * LANES, (hd + 1) * LANES)

    kms = [km_ref[:, cols(hd)] for hd in heads]
    vmts = [_pad_rows(vm_ref[:, cols(hd)].astype(f32), LANES).T.astype(jnp.bfloat16) for hd in heads]

    def with_ones(vt):
        return jnp.concatenate([vt, jnp.ones((SUM_ROWS, vt.shape[1]), vt.dtype)], axis=0)

    def block_rows(ref, qb, hd):
        return ref[pl.ds(pl.multiple_of(qb * bq, bq), bq), cols(hd)]

    def start_block(qb, hd):
        qrows = block_rows(q_ref, qb, hd)
        qcats = []
        for u in range(n_units):
            q = _pad_rows(qrows[u * nq:(u + 1) * nq, :], LANES)
            zero = jnp.zeros_like(q)
            qcats += [jnp.where(comp0, q, zero), jnp.where(comp0, zero, q)]
        qcat = jnp.concatenate(qcats, axis=0)
        qcat_sc[hd] = qcat
        s = lax.dot_general(kms[hd], qcat, _NT, preferred_element_type=f32)
        sm_sc[hd] = s
        mxm_sc[hd] = jnp.max(s, axis=0, keepdims=True)
        scores(0, hd)

    def scores(c, hd):
        s = lax.dot_general(k_ref[c * ck:(c + 1) * ck, cols(hd)], qcat_sc[hd], _NT,
                            preferred_element_type=f32)
        s_sc[hd, c % 2] = s
        mx_sc[hd, c % 2] = jnp.max(s, axis=0, keepdims=True)

    def softmax(c, hd):
        m_old = m_sc[hd]
        m_new = jnp.maximum(m_old, mx_sc[hd, c % 2])
        alpha = jnp.exp2(m_old - m_new)
        p_sc[hd, c % 2] = jnp.exp2(s_sc[hd, c % 2] - m_new).astype(jnp.bfloat16)
        alpha_sc[hd, c % 2] = alpha
        m_sc[hd] = m_new

    def pv(c, hd):
        vt = jnp.concatenate([vt_ref[c * tiles_per_chunk + i, cols(hd), :] for i in range(tiles_per_chunk)],
                             axis=1)
        acc_sc[hd] = alpha_sc[hd, c % 2] * acc_sc[hd] + jnp.dot(with_ones(vt), p_sc[hd, c % 2],
                                                                preferred_element_type=f32)

    def finish_block(qb, hd):
        a_all = acc_sc[hd, :B_VDIM, :] * (1.0 / acc_sc[hd, B_VDIM:B_VDIM + 1, :])
        gate = block_rows(gate_ref, qb, hd)
        outs = []
        for u in range(n_units):
            a = a_all[:, u * 2 * LANES:(u + 1) * 2 * LANES]
            o = a[:, :LANES] - lam * a[:, LANES:]
            ms = jnp.mean(o * o, axis=0, keepdims=True)
            y = (o * lax.rsqrt(ms + EPS) * sg) * (1.0 - lambda_init)
            yt = y.T[:nq, :]
            outs.append((yt * gate[u * nq:(u + 1) * nq, :].astype(f32)).astype(o_ref.dtype))
        o_ref[pl.ds(pl.multiple_of(qb * bq, bq), bq), cols(hd)] = jnp.concatenate(outs, axis=0)

    def block(qb, carry):
        for hd in heads:
            m = mxm_sc[hd]
            p = _pad_rows(jnp.exp2(sm_sc[hd] - m).astype(jnp.bfloat16), LANES)
            m_sc[hd] = m
            acc_sc[hd] = jnp.dot(with_ones(vmts[hd]), p, preferred_element_type=f32)
        for c in range(n_chunks):
            for hd in heads:
                softmax(c, hd)
            if c + 1 < n_chunks:
                for hd in heads:
                    scores(c + 1, hd)
            if c > 0:
                for hd in heads:
                    pv(c - 1, hd)
        for hd in heads:
            pv(n_chunks - 1, hd)
        for hd in heads:
            start_block(jnp.minimum(qb + 1, n_qblocks - 1), hd)
        for hd in heads:
            finish_block(qb, hd)
        return carry

    for hd in heads:
        start_block(0, hd)
    lax.fori_loop(0, n_qblocks, block, 0)


def _diff_attn(lams, sg, q, k, km, vt, vm, gate, *, bsz, seq, tm, q_block, tiles_per_chunk, meta_q,
               lambda_init):
    n_tiles = seq // tm
    tiles_per_chunk = math.gcd(tiles_per_chunk, n_tiles)
    ck = tiles_per_chunk * tm
    n_chunks = n_tiles // tiles_per_chunk
    if meta_q:
        nq, n_units, n_qblocks, n_heads = N_META, 1, 1, B_HEADS
    else:
        nq, n_units, n_qblocks, n_heads = LANES, q_block // LANES, seq // q_block, 1
    q_rows = nq * n_units * n_qblocks
    width = n_units * 2 * LANES
    hw = n_heads * LANES
    kernel = functools.partial(_diff_attn_kernel, n_chunks=n_chunks, tiles_per_chunk=tiles_per_chunk,
                               n_units=n_units, nq=nq, n_qblocks=n_qblocks, n_heads=n_heads,
                               lambda_init=lambda_init)
    lam_spec = pl.BlockSpec((1, HEAD_DIM), lambda b, h: (0, 0))
    bh = lambda b, h: (b, h)
    return pl.pallas_call(
        kernel,
        grid=(bsz, B_HEADS // n_heads),
        in_specs=[
            lam_spec, lam_spec, lam_spec, lam_spec,
            pl.BlockSpec((B_VDIM, 1), lambda b, h: (0, 0)),
            pl.BlockSpec((q_rows, hw), bh),
            pl.BlockSpec((seq, hw), bh),
            pl.BlockSpec((N_META, hw), bh),
            pl.BlockSpec((n_tiles, hw, tm), lambda b, h: (b, h, 0)),
            pl.BlockSpec((N_META, hw), bh),
            pl.BlockSpec((q_rows, hw), bh),
        ],
        out_specs=pl.BlockSpec((q_rows, hw), bh),
        out_shape=jax.ShapeDtypeStruct((bsz * q_rows, B_WIDTH), jnp.bfloat16),
        scratch_shapes=[
            pltpu.VMEM((n_heads, width, LANES), jnp.bfloat16),
            pltpu.VMEM((n_heads, 1, width), jnp.float32),
            pltpu.VMEM((n_heads, B_VDIM + SUM_ROWS, width), jnp.float32),
            pltpu.VMEM((n_heads, 2, ck, width), jnp.float32),
            pltpu.VMEM((n_heads, 2, 1, width), jnp.float32),
            pltpu.VMEM((n_heads, N_META, width), jnp.float32),
            pltpu.VMEM((n_heads, 1, width), jnp.float32),
            pltpu.VMEM((n_heads, 2, ck, width), jnp.bfloat16),
            pltpu.VMEM((n_heads, 2, 1, width), jnp.float32),
        ],
        compiler_params=_cparams(("parallel", "arbitrary")),
        name="diff_attn_meta" if meta_q else "diff_attn",
    )(*lams, sg, q, k, km, vt, vm, gate)


def _window_attn_kernel(sink_ref, q_ref, k_ref, ks_ref, km_ref, kms_ref, vt_ref, vm_ref, gate_ref, o_ref,
                        s_sc, mx_sc, p_sc, m_sc, *, nb, nq, blocks_per_step, meta_q):
    f32 = jnp.float32
    bf = jnp.bfloat16
    step = pl.program_id(1)
    n_keys = s_sc.shape[1]
    n_keys_padded = p_sc.shape[1]
    lane = lax.broadcasted_iota(jnp.int32, (1, LANES), 1)
    lo = lane < HEAD_DIM
    zero = jnp.zeros((LANES, LANES), bf)
    heads = (0, 2, 5, 7, 1, 3, 4, 6)

    km = km_ref[...]
    kms = kms_ref[...]
    _, vmt = _meta_kv(km_ref, vm_ref)
    key_i = lax.broadcasted_iota(jnp.int32, (LANES, LANES), 0)
    qry_i = lax.broadcasted_iota(jnp.int32, (LANES, LANES), 1)
    sink_row = jnp.concatenate([jnp.full((1, LANES), sink_ref[hd] * LOG2E, f32) for hd in heads], axis=1)
    for slot in range(2):
        p_sc[slot, n_keys:, :] = jnp.zeros((n_keys_padded - n_keys, p_sc.shape[2]), bf)

    def rows_of(ref, blk):
        return ref[pl.ds(pl.multiple_of(blk * BLOCK, BLOCK), BLOCK), :]

    def pieces(n):
        if meta_q:
            return ([k_ref[0:BLOCK, :], km], [ks_ref[0:BLOCK, :], kms], [vt_ref[0], vmt],
                    [(N_META + key_i - qry_i) <= WINDOW])
        prev = jnp.maximum(n - 1, 0)
        nxt = jnp.minimum(n + 1, nb - 1)
        return ([rows_of(k_ref, prev), rows_of(k_ref, n), rows_of(k_ref, nxt), km],
                [rows_of(ks_ref, prev), rows_of(ks_ref, n), rows_of(ks_ref, nxt), kms],
                [vt_ref[prev], vt_ref[n], vt_ref[nxt], vmt],
                [(key_i >= qry_i) & (n > 0), None, (key_i <= qry_i) & (n < nb - 1)])

    def scores(j):
        n = step * blocks_per_step + j
        q = _pad_rows(q_ref[j * nq:(j + 1) * nq, :], LANES)
        cols = [q[:, c * LANES:(c + 1) * LANES] for c in range(A_Q // LANES)]
        lo_of = [jnp.where(lo, c, zero) for c in cols]
        hi_of = [jnp.where(lo, zero, c) for c in cols]
        q_plain = jnp.concatenate([lo_of[0], lo_of[1], hi_of[2], hi_of[3]], axis=0)
        q_swap = jnp.concatenate([hi_of[0], hi_of[1], lo_of[2], lo_of[3]], axis=0)
        pk, pks, _, masks = pieces(n)
        s = jnp.concatenate(
            [lax.dot_general(jnp.concatenate(pk, axis=0), q_plain, _NT, preferred_element_type=f32),
             lax.dot_general(jnp.concatenate(pks, axis=0), q_swap, _NT, preferred_element_type=f32)],
            axis=1)
        cols = []
        for t in range(len(heads)):
            sh = s[:, t * LANES:(t + 1) * LANES]
            rows = [sh[r * LANES:(r + 1) * LANES, :] if mask is None
                    else jnp.where(mask, sh[r * LANES:(r + 1) * LANES, :], NEG_INF)
                    for r, mask in enumerate(masks)]
            rows.append(sh[len(masks) * LANES:, :])
            cols.append(jnp.concatenate(rows, axis=0))
        s = jnp.concatenate(cols, axis=1)
        s_sc[j % 2] = s
        mx_sc[j % 2] = jnp.max(s, axis=0, keepdims=True)

    def softmax(j):
        m = jnp.maximum(mx_sc[j % 2], sink_row)
        p_sc[j % 2, :n_keys, :] = jnp.exp2(s_sc[j % 2] - m).astype(bf)
        m_sc[j % 2] = m

    def pv(j):
        n = step * blocks_per_step + j
        _, _, pvt, _ = pieces(n)
        vt = jnp.concatenate(pvt, axis=1)
        vt = jnp.concatenate([vt, jnp.ones((SUM_ROWS, vt.shape[1]), bf)], axis=0)
        ot = jnp.dot(vt, p_sc[j % 2], preferred_element_type=f32)
        l = ot[2 * HEAD_DIM:2 * HEAD_DIM + 1, :] + jnp.exp2(sink_row - m_sc[j % 2])
        ot = ot[:2 * HEAD_DIM, :] * (1.0 / l)
        out_heads = [None] * A_HEADS
        for t, hd in enumerate(heads):
            g = hd // (A_HEADS // A_KV_HEADS)
            out_heads[hd] = ot[g * HEAD_DIM:(g + 1) * HEAD_DIM, t * LANES:(t + 1) * LANES]
        o = jnp.concatenate(out_heads, axis=0).T[:nq, :]
        o_ref[j * nq:(j + 1) * nq, :] = (o * gate_ref[j * nq:(j + 1) * nq, :].astype(f32)).astype(o_ref.dtype)

    scores(0)
    for j in range(blocks_per_step):
        if j + 1 < blocks_per_step:
            scores(j + 1)
        if j > 0:
            pv(j - 1)
        softmax(j)
    pv(blocks_per_step - 1)


def _window_attn(sink, q, k, ks, km, kms, vt, vm, gate, *, bsz, seq, blocks_per_step, meta_q):
    nb = seq // BLOCK
    if meta_q:
        nq, bps, n_steps, n_real_keys = N_META, 1, 1, BLOCK
        k_spec = pl.BlockSpec((BLOCK, A_KV), lambda b, n: (b * nb, 0))
        vt_spec = pl.BlockSpec((1, A_KV, LANES), lambda b, n: (b * nb, 0, 0))
    else:
        bps = math.gcd(blocks_per_step, nb)
        nq, n_steps, n_real_keys = BLOCK, nb // bps, 3 * BLOCK
        k_spec = pl.BlockSpec((seq, A_KV), lambda b, n: (b, 0))
        vt_spec = pl.BlockSpec((nb, A_KV, LANES), lambda b, n: (b, 0, 0))
    q_index = lambda b, n: (b * n_steps + n, 0)
    meta_spec = pl.BlockSpec((N_META, A_KV), lambda b, n: (b, 0))
    return pl.pallas_call(
        functools.partial(_window_attn_kernel, nb=nb, nq=nq, blocks_per_step=bps, meta_q=meta_q),
        grid=(bsz, n_steps),
        in_specs=[
            pl.BlockSpec(memory_space=pltpu.SMEM),
            pl.BlockSpec((bps * nq, A_Q), q_index),
            k_spec, k_spec, meta_spec, meta_spec, vt_spec, meta_spec,
            pl.BlockSpec((bps * nq, A_WIDTH), q_index),
        ],
        out_specs=pl.BlockSpec((bps * nq, A_WIDTH), q_index),
        out_shape=jax.ShapeDtypeStruct((bsz * n_steps * bps * nq, A_WIDTH), jnp.bfloat16),
        scratch_shapes=[
            pltpu.VMEM((2, n_real_keys + N_META, A_HEADS * LANES), jnp.float32),
            pltpu.VMEM((2, 1, A_HEADS * LANES), jnp.float32),
            pltpu.VMEM((2, n_real_keys + LANES, A_HEADS * LANES), jnp.bfloat16),
            pltpu.VMEM((2, 1, A_HEADS * LANES), jnp.float32),
        ],
        compiler_params=_cparams(("parallel", "arbitrary")),
        name="window_attn_meta" if meta_q else "window_attn",
    )(sink, q, k, ks, km, kms, vt, vm, gate)


def _rope_tables(pos):
    inv_freq = 1.0 / (ROPE_THETA ** (jnp.arange(0, HEAD_DIM, 2, dtype=jnp.float32) / HEAD_DIM))
    ang = pos.astype(jnp.float32)[:, None] * inv_freq[None, :]
    ang = jnp.concatenate([ang, ang, ang, ang], axis=-1)
    sign = jnp.where((jnp.arange(LANES) % HEAD_DIM) < HEAD_DIM // 2, -1.0, 1.0).astype(jnp.float32)
    return jnp.cos(ang), jnp.sin(ang) * sign[None, :]


def _encode_both(x_prompt, x_sample, meta_tokens, w_in, w_out, pre_norm_g, post_norm_g, sink_logits,
                 lambda_q1, lambda_k1, lambda_q2, lambda_k2, subln_g, *, tm, q_block):
    d = x_prompt.shape[-1]
    depth = w_in.shape[0]
    assert d == D_MODEL and tm % LANES == 0 and q_block % LANES == 0
    groups = []
    for x in (x_prompt, x_sample):
        bsz, seq, _ = x.shape
        assert seq % tm == 0 and seq % q_block == 0
        groups.append(dict(
            bsz=bsz, seq=seq,
            h_real=x.reshape(bsz * seq, d),
            h_meta=jnp.broadcast_to(meta_tokens[None], (bsz, N_META, d)).reshape(bsz * N_META, d),
            tab_real=_rope_tables(N_META + jnp.arange(seq)),
            tab_meta=_rope_tables(jnp.tile(jnp.arange(N_META), bsz)),
        ))

    for l in range(depth):
        lambda_init = 0.8 - 0.6 * math.exp(-0.3 * l)
        w_i = w_in[l].astype(jnp.bfloat16)
        w_o = w_out[l].astype(jnp.bfloat16)
        pre_g = pre_norm_g[l][None, :]
        post_g = post_norm_g[l][None, :]
        lams = (lambda_q1[l][None, :], lambda_k1[l][None, :], lambda_q2[l][None, :], lambda_k2[l][None, :])
        sg = subln_g[l][:, None]
        for grp in groups:
            bsz, seq = grp["bsz"], grp["seq"]
            real = _proj_call(grp["h_real"], pre_g, w_i, *grp["tab_real"], tm, True, grp.get("prev_real"))
            meta = _proj_call(grp["h_meta"], pre_g, w_i, *grp["tab_meta"], bsz * N_META, False,
                              grp.get("prev_meta"))
            if l > 0:
                grp["h_real"], grp["h_meta"] = real[0], meta[0]
                real, meta = real[1:], meta[1:]
            (aq, ak, aks, avt, ag, bq, bk, bvt, bg) = real
            (maq, mak, maks, mav, mag, mbq, mbk, mbv, mbg) = meta
            win = functools.partial(_window_attn, sink_logits[l], bsz=bsz, seq=seq,
                                    blocks_per_step=WINDOW_BLOCKS_PER_STEP)
            oa = win(aq, ak, aks, mak, maks, avt, mav, ag, meta_q=False)
            moa = win(maq, ak, aks, mak, maks, avt, mav, mag, meta_q=True)
            dif = functools.partial(_diff_attn, lams, sg, bsz=bsz, seq=seq, tm=tm, q_block=q_block,
                                    tiles_per_chunk=DIFF_TILES_PER_CHUNK, lambda_init=lambda_init)
            ob = dif(bq, bk, mbk, bvt, mbv, bg, meta_q=False)
            mob = dif(mbq, bk, mbk, bvt, mbv, mbg, meta_q=True)
            grp["prev_real"] = (oa, ob, w_o[:A_WIDTH], w_o[A_WIDTH:], post_g)
            grp["prev_meta"] = (moa, mob, w_o[:A_WIDTH], w_o[A_WIDTH:], post_g)

    return tuple(_out_proj(*g["prev_real"], g["h_real"], tm).reshape(g["bsz"], g["seq"], d) for g in groups)


def kernel(x_prompt, x_sample, meta_tokens, w_in, w_out, pre_norm_g, post_norm_g, sink_logits,
           lambda_q1, lambda_k1, lambda_q2, lambda_k2, subln_g):
    return _encode_both(x_prompt, x_sample, meta_tokens, w_in, w_out, pre_norm_g, post_norm_g, sink_logits,
                        lambda_q1, lambda_k1, lambda_q2, lambda_k2, subln_g,
                        tm=ROW_TILE, q_block=DIFF_Q_BLOCK)
```

```python
import functools
import math

import jax
import jax.numpy as jnp
from jax import lax
from jax.experimental import pallas as pl
from jax.experimental.pallas import tpu as pltpu

D_MODEL = 1024
N_META = 16
HEAD_DIM = 64
BLOCK = 128
WINDOW = 128
ROPE_THETA = 10000.0
EPS = 1e-6
NEG_INF = -1e30

A_WIDTH = 512
A_HEADS = 8
A_KV_HEADS = 2
B_WIDTH = 512
B_VDIM = 128
B_HEADS = 4
A_Q = A_HEADS * HEAD_DIM
A_KV = A_KV_HEADS * HEAD_DIM
B_QK = 2 * B_HEADS * HEAD_DIM
IN_WIDTH = A_Q + 2 * A_KV + A_WIDTH + 2 * B_QK + 2 * B_WIDTH

OFF_AQ = 0
OFF_AK = OFF_AQ + A_Q
OFF_AV = OFF_AK + A_KV
OFF_AG = OFF_AV + A_KV
OFF_BQ = OFF_AG + A_WIDTH
OFF_BK = OFF_BQ + B_QK
OFF_BV = OFF_BK + B_QK
OFF_BG = OFF_BV + B_WIDTH

LANES = 128
LOG2E = math.log2(math.e)
Q_SCALE = HEAD_DIM ** -0.5 * LOG2E

ROW_TILE = 512
PROJ_ROW_PARTS = 2
DIFF_Q_BLOCK = 512
DIFF_TILES_PER_CHUNK = 1
SUM_ROWS = 16
WINDOW_BLOCKS_PER_STEP = 16
VMEM_LIMIT = 56 * 1024 * 1024

_NT = (((1,), (1,)), ((), ()))


def _cparams(sem):
    return pltpu.CompilerParams(dimension_semantics=sem, vmem_limit_bytes=VMEM_LIMIT)


def _project_rows(x, row0, g_ref, w_ref, cos_ref, sin_ref,
                  aq_ref, ak_ref, aks_ref, av_ref, ag_ref, bq_ref, bk_ref, bv_ref, bg_ref, transposed_v):
    tm = x.shape[0]
    rows = slice(row0, row0 + tm)
    ms = jnp.mean(x * x, axis=-1, keepdims=True)
    hn = (x * lax.rsqrt(ms + EPS) * g_ref[...]).astype(jnp.bfloat16)
    cos = cos_ref[rows, :]
    sin = sin_ref[rows, :]
    lane = lax.broadcasted_iota(jnp.int32, (1, LANES), 1)
    first_half = (lane % HEAD_DIM) < (HEAD_DIM // 2)

    def proj(off, width):
        return jnp.dot(hn, w_ref[:, off:off + width], preferred_element_type=jnp.float32)

    def rope(p):
        fwd = pltpu.roll(p, LANES - HEAD_DIM // 2, 1)
        bwd = pltpu.roll(p, HEAD_DIM // 2, 1)
        return p * cos + jnp.where(first_half, fwd, bwd) * sin

    def silu(p):
        return p * (1.0 / (1.0 + jnp.exp(-p)))

    p = proj(OFF_AQ, A_Q)
    for j in range(A_Q // LANES):
        sl = slice(j * LANES, (j + 1) * LANES)
        aq_ref[rows, sl] = (rope(p[:, sl]) * Q_SCALE).astype(jnp.bfloat16)

    p = proj(OFF_AK, 2 * A_KV)
    k = rope(p[:, :A_KV])
    ak_ref[rows, :] = k.astype(jnp.bfloat16)
    aks_ref[rows, :] = pltpu.roll(k, HEAD_DIM, 1).astype(jnp.bfloat16)
    v = p[:, A_KV:]
    if transposed_v:
        vt = v.T
        for j in range(tm // LANES):
            av_ref[row0 // LANES + j] = vt[:, j * LANES:(j + 1) * LANES].astype(jnp.bfloat16)
    else:
        av_ref[rows, :] = v.astype(jnp.bfloat16)

    ag_ref[rows, :] = silu(proj(OFF_AG, A_WIDTH)).astype(jnp.bfloat16)

    p = proj(OFF_BQ, B_QK)
    for j in range(B_QK // LANES):
        sl = slice(j * LANES, (j + 1) * LANES)
        bq_ref[rows, sl] = (rope(p[:, sl]) * Q_SCALE).astype(jnp.bfloat16)

    p = proj(OFF_BK, B_QK)
    for j in range(B_QK // LANES):
        sl = slice(j * LANES, (j + 1) * LANES)
        bk_ref[rows, sl] = rope(p[:, sl]).astype(jnp.bfloat16)

    p = proj(OFF_BV, B_WIDTH)
    if transposed_v:
        bv_ref[0, :, rows] = p.T.astype(jnp.bfloat16)
    else:
        bv_ref[rows, :] = p.astype(jnp.bfloat16)

    bg_ref[rows, :] = silu(proj(OFF_BG, B_WIDTH)).astype(jnp.bfloat16)


def _row_parts(tm):
    part = tm // PROJ_ROW_PARTS if tm % (PROJ_ROW_PARTS * LANES) == 0 else tm
    return [slice(r, r + part) for r in range(0, tm, part)]


def _in_proj_kernel(h_ref, g_ref, w_ref, cos_ref, sin_ref, *out_refs, transposed_v):
    for rows in _row_parts(h_ref.shape[0]):
        _project_rows(h_ref[rows, :], rows.start, g_ref, w_ref, cos_ref, sin_ref, *out_refs, transposed_v)


def _mid_proj_kernel(oa_ref, ob_ref, wa_ref, wb_ref, post_g_ref, h_ref, g_ref, w_ref, cos_ref, sin_ref,
                     h_out_ref, *out_refs, transposed_v):
    parts = _row_parts(h_ref.shape[0])
    outs = [jnp.dot(oa_ref[rows, :], wa_ref[...], preferred_element_type=jnp.float32)
            + jnp.dot(ob_ref[rows, :], wb_ref[...], preferred_element_type=jnp.float32) for rows in parts]
    for rows, o in zip(parts, outs):
        ms = jnp.mean(o * o, axis=-1, keepdims=True)
        x = h_ref[rows, :] + o * lax.rsqrt(ms + EPS) * post_g_ref[...]
        h_out_ref[rows, :] = x
        _project_rows(x, rows.start, g_ref, w_ref, cos_ref, sin_ref, *out_refs, transposed_v)


def _proj_call(h, pre_g, w_bf16, cos_tab, sin_tab, tm, transposed_v, prev=None):
    rows = h.shape[0]
    nt = rows // tm
    n_tab = cos_tab.shape[0] // tm
    bf = jnp.bfloat16

    def row_spec(width):
        return pl.BlockSpec((tm, width), lambda t: (t, 0))

    def row_shape(width):
        return jax.ShapeDtypeStruct((rows, width), bf)

    if transposed_v:
        av_shape = jax.ShapeDtypeStruct((rows // LANES, A_KV, LANES), bf)
        av_spec = pl.BlockSpec((tm // LANES, A_KV, LANES), lambda t: (t, 0, 0))
        bv_shape = jax.ShapeDtypeStruct((nt, B_WIDTH, tm), bf)
        bv_spec = pl.BlockSpec((1, B_WIDTH, tm), lambda t: (t, 0, 0))
    else:
        av_shape, av_spec = row_shape(A_KV), row_spec(A_KV)
        bv_shape, bv_spec = row_shape(B_WIDTH), row_spec(B_WIDTH)
    out_shape = (row_shape(A_Q), row_shape(A_KV), row_shape(A_KV), av_shape, row_shape(A_WIDTH),
                 row_shape(B_QK), row_shape(B_QK), bv_shape, row_shape(B_WIDTH))
    out_specs = (row_spec(A_Q), row_spec(A_KV), row_spec(A_KV), av_spec, row_spec(A_WIDTH),
                 row_spec(B_QK), row_spec(B_QK), bv_spec, row_spec(B_WIDTH))
    tab_spec = pl.BlockSpec((tm, LANES), lambda t: (t % n_tab, 0))
    h_spec = pl.BlockSpec((tm, D_MODEL), lambda t: (t, 0))
    gain_spec = pl.BlockSpec((1, D_MODEL), lambda t: (0, 0))
    in_specs = [h_spec, gain_spec, pl.BlockSpec((D_MODEL, IN_WIDTH), lambda t: (0, 0)), tab_spec, tab_spec]
    args = (h, pre_g, w_bf16, cos_tab, sin_tab)
    if prev is None:
        kernel, name = _in_proj_kernel, "in_proj"
    else:
        kernel, name = _mid_proj_kernel, "mid_proj"
        oa, ob, wa, wb, post_g = prev
        in_specs = [row_spec(A_WIDTH), row_spec(B_WIDTH),
                    pl.BlockSpec((A_WIDTH, D_MODEL), lambda t: (0, 0)),
                    pl.BlockSpec((B_WIDTH, D_MODEL), lambda t: (0, 0)), gain_spec] + in_specs
        args = (oa, ob, wa, wb, post_g) + args
        out_shape = (jax.ShapeDtypeStruct(h.shape, h.dtype),) + out_shape
        out_specs = (h_spec,) + out_specs
    return pl.pallas_call(
        functools.partial(kernel, transposed_v=transposed_v),
        grid=(nt,), in_specs=in_specs, out_specs=out_specs, out_shape=out_shape,
        compiler_params=_cparams(("parallel",)), name=name if transposed_v else name + "_meta",
    )(*args)


def _out_proj_kernel(oa_ref, ob_ref, wa_ref, wb_ref, g_ref, h_ref, out_ref):
    o = jnp.dot(oa_ref[...], wa_ref[...], preferred_element_type=jnp.float32)
    o = o + jnp.dot(ob_ref[...], wb_ref[...], preferred_element_type=jnp.float32)
    ms = jnp.mean(o * o, axis=-1, keepdims=True)
    out_ref[...] = h_ref[...] + o * lax.rsqrt(ms + EPS) * g_ref[...]


def _out_proj(oa, ob, wa, wb, post_g, h, tm):
    rows = h.shape[0]
    return pl.pallas_call(
        _out_proj_kernel,
        grid=(rows // tm,),
        in_specs=[
            pl.BlockSpec((tm, A_WIDTH), lambda t: (t, 0)),
            pl.BlockSpec((tm, B_WIDTH), lambda t: (t, 0)),
            pl.BlockSpec((A_WIDTH, D_MODEL), lambda t: (0, 0)),
            pl.BlockSpec((B_WIDTH, D_MODEL), lambda t: (0, 0)),
            pl.BlockSpec((1, D_MODEL), lambda t: (0, 0)),
            pl.BlockSpec((tm, D_MODEL), lambda t: (t, 0)),
        ],
        out_specs=pl.BlockSpec((tm, D_MODEL), lambda t: (t, 0)),
        out_shape=jax.ShapeDtypeStruct(h.shape, h.dtype),
        compiler_params=_cparams(("parallel",)), name="out_proj",
    )(oa, ob, wa, wb, post_g, h)


def _pad_rows(x, rows):
    if x.shape[0] == rows:
        return x
    return jnp.concatenate([x, jnp.zeros((rows - x.shape[0], x.shape[1]), x.dtype)], axis=0)


def _meta_kv(km_ref, vm_ref):
    kmp = _pad_rows(km_ref[...], LANES)
    vmt = _pad_rows(vm_ref[...].astype(jnp.float32), LANES).T.astype(jnp.bfloat16)
    return kmp, vmt


def _diff_attn_kernel(lq1_ref, lk1_ref, lq2_ref, lk2_ref, sg_ref,
                      q_ref, k_ref, km_ref, vt_ref, vm_ref, gate_ref, o_ref,
                      qcat_sc, m_sc, acc_sc, s_sc, mx_sc, sm_sc, mxm_sc, p_sc, alpha_sc,
                      *, n_chunks, tiles_per_chunk, n_units, nq, n_qblocks, n_heads, lambda_init):
    f32 = jnp.float32
    bq = n_units * nq
    ck = tiles_per_chunk * vt_ref.shape[2]
    lam = (jnp.exp(jnp.sum(lq1_ref[...] * lk1_ref[...], axis=-1, keepdims=True))
           - jnp.exp(jnp.sum(lq2_ref[...] * lk2_ref[...], axis=-1, keepdims=True)) + lambda_init)
    sg = sg_ref[...]
    lane = lax.broadcasted_iota(jnp.int32, (1, LANES), 1)
    comp0 = lane < HEAD_DIM
    heads = range(n_heads)

    def cols(hd):
        return slice(hd * LANES, (hd + 1) * LANES)

    kms = [km_ref[:, cols(hd)] for hd in heads]
    vmts = [_pad_rows(vm_ref[:, cols(hd)].astype(f32), LANES).T.astype(jnp.bfloat16) for hd in heads]

    def with_ones(vt):
        return jnp.concatenate([vt, jnp.ones((SUM_ROWS, vt.shape[1]), vt.dtype)], axis=0)

    def block_rows(ref, qb, hd):
        return ref[pl.ds(pl.multiple_of(qb * bq, bq), bq), cols(hd)]

    def start_block(qb, hd):
        qrows = block_rows(q_ref, qb, hd)
        qcats = []
        for u in range(n_units):
            q = _pad_rows(qrows[u * nq:(u + 1) * nq, :], LANES)
            zero = jnp.zeros_like(q)
            qcats += [jnp.where(comp0, q, zero), jnp.where(comp0, zero, q)]
        qcat = jnp.concatenate(qcats, axis=0)
        qcat_sc[hd] = qcat
        s = lax.dot_general(kms[hd], qcat, _NT, preferred_element_type=f32)
        sm_sc[hd] = s
        mxm_sc[hd] = jnp.max(s, axis=0, keepdims=True)
        scores(0, hd)

    def scores(c, hd):
        s = lax.dot_general(k_ref[c * ck:(c + 1) * ck, cols(hd)], qcat_sc[hd], _NT,
                            preferred_element_type=f32)
        s_sc[hd, c % 2] = s
        mx_sc[hd, c % 2] = jnp.max(s, axis=0, keepdims=True)

    def softmax(c, hd):
        m_old = m_sc[hd]
        m_new = jnp.maximum(m_old, mx_sc[hd, c % 2])
        alpha = jnp.exp2(m_old - m_new)
        p_sc[hd, c % 2] = jnp.exp2(s_sc[hd, c % 2] - m_new).astype(jnp.bfloat16)
        alpha_sc[hd, c % 2] = alpha
        m_sc[hd] = m_new

    def pv(c, hd):
        vt = jnp.concatenate([vt_ref[c * tiles_per_chunk + i, cols(hd), :] for i in range(tiles_per_chunk)],
                             axis=1)
        acc_sc[hd] = alpha_sc[hd, c % 2] * acc_sc[hd] + jnp.dot(with_ones(vt), p_sc[hd, c % 2],
                                                                preferred_element_type=f32)

    def finish_block(qb, hd):
        a_all = acc_sc[hd, :B_VDIM, :] * (1.0 / acc_sc[hd, B_VDIM:B_VDIM + 1, :])
        gate = block_rows(gate_ref, qb, hd)
        outs = []
        for u in range(n_units):
            a = a_all[:, u * 2 * LANES:(u + 1) * 2 * LANES]
            o = a[:, :LANES] - lam * a[:, LANES:]
            ms = jnp.mean(o * o, axis=0, keepdims=True)
            y = (o * lax.rsqrt(ms + EPS) * sg) * (1.0 - lambda_init)
            yt = y.T[:nq, :]
            outs.append((yt * gate[u * nq:(u + 1) * nq, :].astype(f32)).astype(o_ref.dtype))
        o_ref[pl.ds(pl.multiple_of(qb * bq, bq), bq), cols(hd)] = jnp.concatenate(outs, axis=0)

    def block(qb, carry):
        for hd in heads:
            m = mxm_sc[hd]
            p = _pad_rows(jnp.exp2(sm_sc[hd] - m).astype(jnp.bfloat16), LANES)
            m_sc[hd] = m
            acc_sc[hd] = jnp.dot(with_ones(vmts[hd]), p, preferred_element_type=f32)
        for c in range(n_chunks):
            for hd in heads:
                softmax(c, hd)
            if c + 1 < n_chunks:
                for hd in heads:
                    scores(c + 1, hd)
            if c > 0:
                for hd in heads:
                    pv(c - 1, hd)
        for hd in heads:
            pv(n_chunks - 1, hd)
        for hd in heads:
            start_block(jnp.minimum(qb + 1, n_qblocks - 1), hd)
        for hd in heads:
            finish_block(qb, hd)
        return carry

    for hd in heads:
        start_block(0, hd)
    lax.fori_loop(0, n_qblocks, block, 0)


def _diff_attn(lams, sg, q, k, km, vt, vm, gate, *, bsz, seq, tm, q_block, tiles_per_chunk, meta_q,
               lambda_init):
    n_tiles = seq // tm
    tiles_per_chunk = math.gcd(tiles_per_chunk, n_tiles)
    ck = tiles_per_chunk * tm
    n_chunks = n_tiles // tiles_per_chunk
    if meta_q:
        nq, n_units, n_qblocks, n_heads = N_META, 1, 1, B_HEADS
    else:
        nq, n_units, n_qblocks, n_heads = LANES, q_block // LANES, seq // q_block, 1
    q_rows = nq * n_units * n_qblocks
    width = n_units * 2 * LANES
    hw = n_heads * LANES
    kernel = functools.partial(_diff_attn_kernel, n_chunks=n_chunks, tiles_per_chunk=tiles_per_chunk,
                               n_units=n_units, nq=nq, n_qblocks=n_qblocks, n_heads=n_heads,
                               lambda_init=lambda_init)
    lam_spec = pl.BlockSpec((1, HEAD_DIM), lambda b, h: (0, 0))
    bh = lambda b, h: (b, h)
    return pl.pallas_call(
        kernel,
        grid=(bsz, B_HEADS // n_heads),
        in_specs=[
            lam_spec, lam_spec, lam_spec, lam_spec,
            pl.BlockSpec((B_VDIM, 1), lambda b, h: (0, 0)),
            pl.BlockSpec((q_rows, hw), bh),
            pl.BlockSpec((seq, hw), bh),
            pl.BlockSpec((N_META, hw), bh),
            pl.BlockSpec((n_tiles, hw, tm), lambda b, h: (b, h, 0)),
            pl.BlockSpec((N_META, hw), bh),
            pl.BlockSpec((q_rows, hw), bh),
        ],
        out_specs=pl.BlockSpec((q_rows, hw), bh),
        out_shape=jax.ShapeDtypeStruct((bsz * q_rows, B_WIDTH), jnp.bfloat16),
        scratch_shapes=[
            pltpu.VMEM((n_heads, width, LANES), jnp.bfloat16),
            pltpu.VMEM((n_heads, 1, width), jnp.float32),
            pltpu.VMEM((n_heads, B_VDIM + SUM_ROWS, width), jnp.float32),
            pltpu.VMEM((n_heads, 2, ck, width), jnp.float32),
            pltpu.VMEM((n_heads, 2, 1, width), jnp.float32),
            pltpu.VMEM((n_heads, N_META, width), jnp.float32),
            pltpu.VMEM((n_heads, 1, width), jnp.float32),
            pltpu.VMEM((n_heads, 2, ck, width), jnp.bfloat16),
            pltpu.VMEM((n_heads, 2, 1, width), jnp.float32),
        ],
        compiler_params=_cparams(("parallel", "arbitrary")),
        name="diff_attn_meta" if meta_q else "diff_attn",
    )(*lams, sg, q, k, km, vt, vm, gate)


def _window_attn_kernel(sink_ref, q_ref, k_ref, ks_ref, km_ref, kms_ref, vt_ref, vm_ref, gate_ref, o_ref,
                        s_sc, mx_sc, p_sc, m_sc, *, nb, nq, blocks_per_step, meta_q):
    f32 = jnp.float32
    bf = jnp.bfloat16
    step = pl.program_id(1)
    n_keys = s_sc.shape[1]
    n_keys_padded = p_sc.shape[1]
    lane = lax.broadcasted_iota(jnp.int32, (1, LANES), 1)
    lo = lane < HEAD_DIM
    zero = jnp.zeros((LANES, LANES), bf)
    heads = (0, 2, 5, 7, 1, 3, 4, 6)

    km = km_ref[...]
    kms = kms_ref[...]
    _, vmt = _meta_kv(km_ref, vm_ref)
    key_i = lax.broadcasted_iota(jnp.int32, (LANES, LANES), 0)
    qry_i = lax.broadcasted_iota(jnp.int32, (LANES, LANES), 1)
    sink_row = jnp.concatenate([jnp.full((1, LANES), sink_ref[hd] * LOG2E, f32) for hd in heads], axis=1)
    for slot in range(2):
        p_sc[slot, n_keys:, :] = jnp.zeros((n_keys_padded - n_keys, p_sc.shape[2]), bf)

    def rows_of(ref, blk):
        return ref[pl.ds(pl.multiple_of(blk * BLOCK, BLOCK), BLOCK), :]

    def pieces(n):
        if meta_q:
            return ([k_ref[0:BLOCK, :], km], [ks_ref[0:BLOCK, :], kms], [vt_ref[0], vmt],
                    [(N_META + key_i - qry_i) <= WINDOW])
        prev = jnp.maximum(n - 1, 0)
        nxt = jnp.minimum(n + 1, nb - 1)
        return ([rows_of(k_ref, prev), rows_of(k_ref, n), rows_of(k_ref, nxt), km],
                [rows_of(ks_ref, prev), rows_of(ks_ref, n), rows_of(ks_ref, nxt), kms],
                [vt_ref[prev], vt_ref[n], vt_ref[nxt], vmt],
                [(key_i >= qry_i) & (n > 0), None, (key_i <= qry_i) & (n < nb - 1)])

    def scores(j):
        n = step * blocks_per_step + j
        q = _pad_rows(q_ref[j * nq:(j + 1) * nq, :], LANES)
        cols = [q[:, c * LANES:(c + 1) * LANES] for c in range(A_Q // LANES)]
        lo_of = [jnp.where(lo, c, zero) for c in cols]
        hi_of = [jnp.where(lo, zero, c) for c in cols]
        q_plain = jnp.concatenate([lo_of[0], lo_of[1], hi_of[2], hi_of[3]], axis=0)
        q_swap = jnp.concatenate([hi_of[0], hi_of[1], lo_of[2], lo_of[3]], axis=0)
        pk, pks, _, masks = pieces(n)
        s = jnp.concatenate(
            [lax.dot_general(jnp.concatenate(pk, axis=0), q_plain, _NT, preferred_element_type=f32),
             lax.dot_general(jnp.concatenate(pks, axis=0), q_swap, _NT, preferred_element_type=f32)],
            axis=1)
        cols = []
        for t in range(len(heads)):
            sh = s[:, t * LANES:(t + 1) * LANES]
            rows = [sh[r * LANES:(r + 1) * LANES, :] if mask is None
                    else jnp.where(mask, sh[r * LANES:(r + 1) * LANES, :], NEG_INF)
                    for r, mask in enumerate(masks)]
            rows.append(sh[len(masks) * LANES:, :])
            cols.append(jnp.concatenate(rows, axis=0))
        s = jnp.concatenate(cols, axis=1)
        s_sc[j % 2] = s
        mx_sc[j % 2] = jnp.max(s, axis=0, keepdims=True)

    def softmax(j):
        m = jnp.maximum(mx_sc[j % 2], sink_row)
        p_sc[j % 2, :n_keys, :] = jnp.exp2(s_sc[j % 2] - m).astype(bf)
        m_sc[j % 2] = m

    def pv(j):
        n = step * blocks_per_step + j
        _, _, pvt, _ = pieces(n)
        vt = jnp.concatenate(pvt, axis=1)
        vt = jnp.concatenate([vt, jnp.ones((SUM_ROWS, vt.shape[1]), bf)], axis=0)
        ot = jnp.dot(vt, p_sc[j % 2], preferred_element_type=f32)
        l = ot[2 * HEAD_DIM:2 * HEAD_DIM + 1, :] + jnp.exp2(sink_row - m_sc[j % 2])
        ot = ot[:2 * HEAD_DIM, :] * (1.0 / l)
        out_heads = [None] * A_HEADS
        for t, hd in enumerate(heads):
            g = hd // (A_HEADS // A_KV_HEADS)
            out_heads[hd] = ot[g * HEAD_DIM:(g + 1) * HEAD_DIM, t * LANES:(t + 1) * LANES]
        o = jnp.concatenate(out_heads, axis=0).T[:nq, :]
        o_ref[j * nq:(j + 1) * nq, :] = (o * gate_ref[j * nq:(j + 1) * nq, :].astype(f32)).astype(o_ref.dtype)

    scores(0)
    for j in range(blocks_per_step):
        if j + 1 < blocks_per_step:
            scores(j + 1)
        if j > 0:
            pv(j - 1)
        softmax(j)
    pv(blocks_per_step - 1)


def _window_attn(sink, q, k, ks, km, kms, vt, vm, gate, *, bsz, seq, blocks_per_step, meta_q):
    nb = seq // BLOCK
    if meta_q:
        nq, bps, n_steps, n_real_keys = N_META, 1, 1, BLOCK
        k_spec = pl.BlockSpec((BLOCK, A_KV), lambda b, n: (b * nb, 0))
        vt_spec = pl.BlockSpec((1, A_KV, LANES), lambda b, n: (b * nb, 0, 0))
    else:
        bps = math.gcd(blocks_per_step, nb)
        nq, n_steps, n_real_keys = BLOCK, nb // bps, 3 * BLOCK
        k_spec = pl.BlockSpec((seq, A_KV), lambda b, n: (b, 0))
        vt_spec = pl.BlockSpec((nb, A_KV, LANES), lambda b, n: (b, 0, 0))
    q_index = lambda b, n: (b * n_steps + n, 0)
    meta_spec = pl.BlockSpec((N_META, A_KV), lambda b, n: (b, 0))
    return pl.pallas_call(
        functools.partial(_window_attn_kernel, nb=nb, nq=nq, blocks_per_step=bps, meta_q=meta_q),
        grid=(bsz, n_steps),
        in_specs=[
            pl.BlockSpec(memory_space=pltpu.SMEM),
            pl.BlockSpec((bps * nq, A_Q), q_index),
            k_spec, k_spec, meta_spec, meta_spec, vt_spec, meta_spec,
            pl.BlockSpec((bps * nq, A_WIDTH), q_index),
        ],
        out_specs=pl.BlockSpec((bps * nq, A_WIDTH), q_index),
        out_shape=jax.ShapeDtypeStruct((bsz * n_steps * bps * nq, A_WIDTH), jnp.bfloat16),
        scratch_shapes=[
            pltpu.VMEM((2, n_real_keys + N_META, A_HEADS * LANES), jnp.float32),
            pltpu.VMEM((2, 1, A_HEADS * LANES), jnp.float32),
            pltpu.VMEM((2, n_real_keys + LANES, A_HEADS * LANES), jnp.bfloat16),
            pltpu.VMEM((2, 1, A_HEADS * LANES), jnp.float32),
        ],
        compiler_params=_cparams(("parallel", "arbitrary")),
        name="window_attn_meta" if meta_q else "window_attn",
    )(sink, q, k, ks, km, kms, vt, vm, gate)


def _rope_tables(pos):
    inv_freq = 1.0 / (ROPE_THETA ** (jnp.arange(0, HEAD_DIM, 2, dtype=jnp.float32) / HEAD_DIM))
    ang = pos.astype(jnp.float32)[:, None] * inv_freq[None, :]
    ang = jnp.concatenate([ang, ang, ang, ang], axis=-1)
    sign = jnp.where((jnp.arange(LANES) % HEAD_DIM) < HEAD_DIM // 2, -1.0, 1.0).astype(jnp.float32)
    return jnp.cos(ang), jnp.sin(ang) * sign[None, :]


def _encode_both(x_prompt, x_sample, meta_tokens, w_in, w_out, pre_norm_g, post_norm_g, sink_logits,
                 lambda_q1, lambda_k1, lambda_q2, lambda_k2, subln_g, *, tm, q_block):
    d = x_prompt.shape[-1]
    depth = w_in.shape[0]
    assert d == D_MODEL and tm % LANES == 0 and q_block % LANES == 0
    groups = []
    for x in (x_prompt, x_sample):
        bsz, seq, _ = x.shape
        assert seq % tm == 0 and seq % q_block == 0
        groups.append(dict(
            bsz=bsz, seq=seq,
            h_real=x.reshape(bsz * seq, d),
            h_meta=jnp.broadcast_to(meta_tokens[None], (bsz, N_META, d)).reshape(bsz * N_META, d),
            tab_real=_rope_tables(N_META + jnp.arange(seq)),
            tab_meta=_rope_tables(jnp.tile(jnp.arange(N_META), bsz)),
        ))

    for l in range(depth):
        lambda_init = 0.8 - 0.6 * math.exp(-0.3 * l)
        w_i = w_in[l].astype(jnp.bfloat16)
        w_o = w_out[l].astype(jnp.bfloat16)
        pre_g = pre_norm_g[l][None, :]
        post_g = post_norm_g[l][None, :]
        lams = (lambda_q1[l][None, :], lambda_k1[l][None, :], lambda_q2[l][None, :], lambda_k2[l][None, :])
        sg = subln_g[l][:, None]
        for grp in groups:
            bsz, seq = grp["bsz"], grp["seq"]
            real = _proj_call(grp["h_real"], pre_g, w_i, *grp["tab_real"], tm, True, grp.get("prev_real"))
            meta = _proj_call(grp["h_meta"], pre_g, w_i, *grp["tab_meta"], bsz * N_META, False,
                              grp.get("prev_meta"))
            if l > 0:
                grp["h_real"], grp["h_meta"] = real[0], meta[0]
                real, meta = real[1:], meta[1:]
            (aq, ak, aks, avt, ag, bq, bk, bvt, bg) = real
            (maq, mak, maks, mav, mag, mbq, mbk, mbv, mbg) = meta
            win = functools.partial(_window_attn, sink_logits[l], bsz=bsz, seq=seq,
                                    blocks_per_step=WINDOW_BLOCKS_PER_STEP)
            oa = win(aq, ak, aks, mak, maks, avt, mav, ag, meta_q=False)
            moa = win(maq, ak, aks, mak, maks, avt, mav, mag, meta_q=True)
            dif = functools.partial(_diff_attn, lams, sg, bsz=bsz, seq=seq, tm=tm, q_block=q_block,
                                    tiles_per_chunk=DIFF_TILES_PER_CHUNK, lambda_init=lambda_init)
            ob = dif(bq, bk, mbk, bvt, mbv, bg, meta_q=False)
            mob = dif(mbq, bk, mbk, bvt, mbv, mbg, meta_q=True)
            grp["prev_real"] = (oa, ob, w_o[:A_WIDTH], w_o[A_WIDTH:], post_g)
            grp["prev_meta"] = (moa, mob, w_o[:A_WIDTH], w_o[A_WIDTH:], post_g)

    return tuple(_out_proj(*g["prev_real"], g["h_real"], tm).reshape(g["bsz"], g["seq"], d) for g in groups)


def kernel(x_prompt, x_sample, meta_tokens, w_in, w_out, pre_norm_g, post_norm_g, sink_logits,
           lambda_q1, lambda_k1, lambda_q2, lambda_k2, subln_g):
    return _encode_both(x_prompt, x_sample, meta_tokens, w_in, w_out, pre_norm_g, post_norm_g, sink_logits,
                        lambda_q1, lambda_k1, lambda_q2, lambda_k2, subln_g,
                        tm=ROW_TILE, q_block=DIFF_Q_BLOCK)
```

```python
import functools
import math

import jax
import jax.numpy as jnp
from jax import lax
from jax.experimental import pallas as pl
from jax.experimental.pallas import tpu as pltpu

D_MODEL = 1024
N_META = 16
HEAD_DIM = 64
BLOCK = 128
WINDOW = 128
ROPE_THETA = 10000.0
EPS = 1e-6
NEG_INF = -1e30

A_WIDTH = 512
A_HEADS = 8
A_KV_HEADS = 2
B_WIDTH = 512
B_VDIM = 128
B_HEADS = 4
A_Q = A_HEADS * HEAD_DIM
A_KV = A_KV_HEADS * HEAD_DIM
B_QK = 2 * B_HEADS * HEAD_DIM
IN_WIDTH = A_Q + 2 * A_KV + A_WIDTH + 2 * B_QK + 2 * B_WIDTH

OFF_AQ = 0
OFF_AK = OFF_AQ + A_Q
OFF_AV = OFF_AK + A_KV
OFF_AG = OFF_AV + A_KV
OFF_BQ = OFF_AG + A_WIDTH
OFF_BK = OFF_BQ + B_QK
OFF_BV = OFF_BK + B_QK
OFF_BG = OFF_BV + B_WIDTH

LANES = 128
LOG2E = math.log2(math.e)
Q_SCALE = HEAD_DIM ** -0.5 * LOG2E

ROW_TILE = 512
PROJ_ROW_PARTS = 2
DIFF_Q_BLOCK = 512
DIFF_TILES_PER_CHUNK = 1
SUM_ROWS = 16
WINDOW_BLOCKS_PER_STEP = 16
VMEM_LIMIT = 56 * 1024 * 1024

_NT = (((1,), (1,)), ((), ()))


def _cparams(sem):
    return pltpu.CompilerParams(dimension_semantics=sem, vmem_limit_bytes=VMEM_LIMIT)


def _project_rows(x, row0, g_ref, w_ref, cos_ref, sin_ref,
                  aq_ref, ak_ref, aks_ref, av_ref, ag_ref, bq_ref, bk_ref, bv_ref, bg_ref, transposed_v):
    tm = x.shape[0]
    rows = slice(row0, row0 + tm)
    ms = jnp.mean(x * x, axis=-1, keepdims=True)
    hn = (x * lax.rsqrt(ms + EPS) * g_ref[...]).astype(jnp.bfloat16)
    cos = cos_ref[rows, :]
    sin = sin_ref[rows, :]
    lane = lax.broadcasted_iota(jnp.int32, (1, LANES), 1)
    first_half = (lane % HEAD_DIM) < (HEAD_DIM // 2)

    def proj(off, width):
        return jnp.dot(hn, w_ref[:, off:off + width], preferred_element_type=jnp.float32)

    def rope(p):
        fwd = pltpu.roll(p, LANES - HEAD_DIM // 2, 1)
        bwd = pltpu.roll(p, HEAD_DIM // 2, 1)
        return p * cos + jnp.where(first_half, fwd, bwd) * sin

    def silu(p):
        return p * (1.0 / (1.0 + jnp.exp(-p)))

    p = proj(OFF_AQ, A_Q)
    for j in range(A_Q // LANES):
        sl = slice(j * LANES, (j + 1) * LANES)
        aq_ref[rows, sl] = (rope(p[:, sl]) * Q_SCALE).astype(jnp.bfloat16)

    p = proj(OFF_AK, 2 * A_KV)
    k = rope(p[:, :A_KV])
    ak_ref[rows, :] = k.astype(jnp.bfloat16)
    aks_ref[rows, :] = pltpu.roll(k, HEAD_DIM, 1).astype(jnp.bfloat16)
    v = p[:, A_KV:]
    if transposed_v:
        vt = v.T
        for j in range(tm // LANES):
            av_ref[row0 // LANES + j] = vt[:, j * LANES:(j + 1) * LANES].astype(jnp.bfloat16)
    else:
        av_ref[rows, :] = v.astype(jnp.bfloat16)

    ag_ref[rows, :] = silu(proj(OFF_AG, A_WIDTH)).astype(jnp.bfloat16)

    p = proj(OFF_BQ, B_QK)
    for j in range(B_QK // LANES):
        sl = slice(j * LANES, (j + 1) * LANES)
        bq_ref[rows, sl] = (rope(p[:, sl]) * Q_SCALE).astype(jnp.bfloat16)

    p = proj(OFF_BK, B_QK)
    for j in range(B_QK // LANES):
        sl = slice(j * LANES, (j + 1) * LANES)
        bk_ref[rows, sl] = rope(p[:, sl]).astype(jnp.bfloat16)

    p = proj(OFF_BV, B_WIDTH)
    if transposed_v:
        bv_ref[0, :, rows] = p.T.astype(jnp.bfloat16)
    else:
        bv_ref[rows, :] = p.astype(jnp.bfloat16)

    bg_ref[rows, :] = silu(proj(OFF_BG, B_WIDTH)).astype(jnp.bfloat16)


def _row_parts(tm):
    part = tm // PROJ_ROW_PARTS if tm % (PROJ_ROW_PARTS * LANES) == 0 else tm
    return [slice(r, r + part) for r in range(0, tm, part)]


def _in_proj_kernel(h_ref, g_ref, w_ref, cos_ref, sin_ref, *out_refs, transposed_v):
    for rows in _row_parts(h_ref.shape[0]):
        _project_rows(h_ref[rows, :], rows.start, g_ref, w_ref, cos_ref, sin_ref, *out_refs, transposed_v)


def _mid_proj_kernel(oa_ref, ob_ref, wa_ref, wb_ref, post_g_ref, h_ref, g_ref, w_ref, cos_ref, sin_ref,
                     h_out_ref, *out_refs, transposed_v):
    parts = _row_parts(h_ref.shape[0])
    outs = [jnp.dot(oa_ref[rows, :], wa_ref[...], preferred_element_type=jnp.float32)
            + jnp.dot(ob_ref[rows, :], wb_ref[...], preferred_element_type=jnp.float32) for rows in parts]
    for rows, o in zip(parts, outs):
        ms = jnp.mean(o * o, axis=-1, keepdims=True)
        x = h_ref[rows, :] + o * lax.rsqrt(ms + EPS) * post_g_ref[...]
        h_out_ref[rows, :] = x
        _project_rows(x, rows.start, g_ref, w_ref, cos_ref, sin_ref, *out_refs, transposed_v)


def _proj_call(h, pre_g, w_bf16, cos_tab, sin_tab, tm, transposed_v, prev=None):
    rows = h.shape[0]
    nt = rows // tm
    n_tab = cos_tab.shape[0] // tm
    bf = jnp.bfloat16

    def row_spec(width):
        return pl.BlockSpec((tm, width), lambda t: (t, 0))

    def row_shape(width):
        return jax.ShapeDtypeStruct((rows, width), bf)

    if transposed_v:
        av_shape = jax.ShapeDtypeStruct((rows // LANES, A_KV, LANES), bf)
        av_spec = pl.BlockSpec((tm // LANES, A_KV, LANES), lambda t: (t, 0, 0))
        bv_shape = jax.ShapeDtypeStruct((nt, B_WIDTH, tm), bf)
        bv_spec = pl.BlockSpec((1, B_WIDTH, tm), lambda t: (t, 0, 0))
    else:
        av_shape, av_spec = row_shape(A_KV), row_spec(A_KV)
        bv_shape, bv_spec = row_shape(B_WIDTH), row_spec(B_WIDTH)
    out_shape = (row_shape(A_Q), row_shape(A_KV), row_shape(A_KV), av_shape, row_shape(A_WIDTH),
                 row_shape(B_QK), row_shape(B_QK), bv_shape, row_shape(B_WIDTH))
    out_specs = (row_spec(A_Q), row_spec(A_KV), row_spec(A_KV), av_spec, row_spec(A_WIDTH),
                 row_spec(B_QK), row_spec(B_QK), bv_spec, row_spec(B_WIDTH))
    tab_spec = pl.BlockSpec((tm, LANES), lambda t: (t % n_tab, 0))
    h_spec = pl.BlockSpec((tm, D_MODEL), lambda t: (t, 0))
    gain_spec = pl.BlockSpec((1, D_MODEL), lambda t: (0, 0))
    in_specs = [h_spec, gain_spec, pl.BlockSpec((D_MODEL, IN_WIDTH), lambda t: (0, 0)), tab_spec, tab_spec]
    args = (h, pre_g, w_bf16, cos_tab, sin_tab)
    if prev is None:
        kernel, name = _in_proj_kernel, "in_proj"
    else:
        kernel, name = _mid_proj_kernel, "mid_proj"
        oa, ob, wa, wb, post_g = prev
        in_specs = [row_spec(A_WIDTH), row_spec(B_WIDTH),
                    pl.BlockSpec((A_WIDTH, D_MODEL), lambda t: (0, 0)),
                    pl.BlockSpec((B_WIDTH, D_MODEL), lambda t: (0, 0)), gain_spec] + in_specs
        args = (oa, ob, wa, wb, post_g) + args
        out_shape = (jax.ShapeDtypeStruct(h.shape, h.dtype),) + out_shape
        out_specs = (h_spec,) + out_specs
    return pl.pallas_call(
        functools.partial(kernel, transposed_v=transposed_v),
        grid=(nt,), in_specs=in_specs, out_specs=out_specs, out_shape=out_shape,
        compiler_params=_cparams(("parallel",)), name=name if transposed_v else name + "_meta",
    )(*args)


def _out_proj_kernel(oa_ref, ob_ref, wa_ref, wb_ref, g_ref, h_ref, out_ref):
    o = jnp.dot(oa_ref[...], wa_ref[...], preferred_element_type=jnp.float32)
    o = o + jnp.dot(ob_ref[...], wb_ref[...], preferred_element_type=jnp.float32)
    ms = jnp.mean(o * o, axis=-1, keepdims=True)
    out_ref[...] = h_ref[...] + o * lax.rsqrt(ms + EPS) * g_ref[...]


def _out_proj(oa, ob, wa, wb, post_g, h, tm):
    rows = h.shape[0]
    return pl.pallas_call(
        _out_proj_kernel,
        grid=(rows // tm,),
        in_specs=[
            pl.BlockSpec((tm, A_WIDTH), lambda t: (t, 0)),
            pl.BlockSpec((tm, B_WIDTH), lambda t: (t, 0)),
            pl.BlockSpec((A_WIDTH, D_MODEL), lambda t: (0, 0)),
            pl.BlockSpec((B_WIDTH, D_MODEL), lambda t: (0, 0)),
            pl.BlockSpec((1, D_MODEL), lambda t: (0, 0)),
            pl.BlockSpec((tm, D_MODEL), lambda t: (t, 0)),
        ],
        out_specs=pl.BlockSpec((tm, D_MODEL), lambda t: (t, 0)),
        out_shape=jax.ShapeDtypeStruct(h.shape, h.dtype),
        compiler_params=_cparams(("parallel",)), name="out_proj",
    )(oa, ob, wa, wb, post_g, h)


def _pad_rows(x, rows):
    if x.shape[0] == rows:
        return x
    return jnp.concatenate([x, jnp.zeros((rows - x.shape[0], x.shape[1]), x.dtype)], axis=0)


def _meta_kv(km_ref, vm_ref):
    kmp = _pad_rows(km_ref[...], LANES)
    vmt = _pad_rows(vm_ref[...].astype(jnp.float32), LANES).T.astype(jnp.bfloat16)
    return kmp, vmt


def _diff_attn_kernel(lq1_ref, lk1_ref, lq2_ref, lk2_ref, sg_ref,
                      q_ref, k_ref, km_ref, vt_ref, vm_ref, gate_ref, o_ref,
                      qcat_sc, m_sc, acc_sc, s_sc, mx_sc, sm_sc, mxm_sc, p_sc, alpha_sc,
                      *, n_chunks, tiles_per_chunk, n_units, nq, n_qblocks, n_heads, lambda_init):
    f32 = jnp.float32
    bq = n_units * nq
    ck = tiles_per_chunk * vt_ref.shape[2]
    lam = (jnp.exp(jnp.sum(lq1_ref[...] * lk1_ref[...], axis=-1, keepdims=True))
           - jnp.exp(jnp.sum(lq2_ref[...] * lk2_ref[...], axis=-1, keepdims=True)) + lambda_init)
    sg = sg_ref[...]
    lane = lax.broadcasted_iota(jnp.int32, (1, LANES), 1)
    comp0 = lane < HEAD_DIM
    heads = range(n_heads)

    def cols(hd):
        return slice(hd * LANES, (hd + 1) * LANES)

    kms = [km_ref[:, cols(hd)] for hd in heads]
    vmts = [_pad_rows(vm_ref[:, cols(hd)].astype(f32), LANES).T.astype(jnp.bfloat16) for hd in heads]

    def with_ones(vt):
        return jnp.concatenate([vt, jnp.ones((SUM_ROWS, vt.shape[1]), vt.dtype)], axis=0)

    def block_rows(ref, qb, hd):
        return ref[pl.ds(pl.multiple_of(qb * bq, bq), bq), cols(hd)]

    def start_block(qb, hd):
        qrows = block_rows(q_ref, qb, hd)
        qcats = []
        for u in range(n_units):
            q = _pad_rows(qrows[u * nq:(u + 1) * nq, :], LANES)
            zero = jnp.zeros_like(q)
            qcats += [jnp.where(comp0, q, zero), jnp.where(comp0, zero, q)]
        qcat = jnp.concatenate(qcats, axis=0)
        qcat_sc[hd] = qcat
        s = lax.dot_general(kms[hd], qcat, _NT, preferred_element_type=f32)
        sm_sc[hd] = s
        mxm_sc[hd] = jnp.max(s, axis=0, keepdims=True)
        scores(0, hd)

    def scores(c, hd):
        s = lax.dot_general(k_ref[c * ck:(c + 1) * ck, cols(hd)], qcat_sc[hd], _NT,
                            preferred_element_type=f32)
        s_sc[hd, c % 2] = s
        mx_sc[hd, c % 2] = jnp.max(s, axis=0, keepdims=True)

    def softmax(c, hd):
        m_old = m_sc[hd]
        m_new = jnp.maximum(m_old, mx_sc[hd, c % 2])
        alpha = jnp.exp2(m_old - m_new)
        p_sc[hd, c % 2] = jnp.exp2(s_sc[hd, c % 2] - m_new).astype(jnp.bfloat16)
        alpha_sc[hd, c % 2] = alpha
        m_sc[hd] = m_new

    def pv(c, hd):
        vt = jnp.concatenate([vt_ref[c * tiles_per_chunk + i, cols(hd), :] for i in range(tiles_per_chunk)],
                             axis=1)
        acc_sc[hd] = alpha_sc[hd, c % 2] * acc_sc[hd] + jnp.dot(with_ones(vt), p_sc[hd, c % 2],
                                                                preferred_element_type=f32)

    def finish_block(qb, hd):
        a_all = acc_sc[hd, :B_VDIM, :] * (1.0 / acc_sc[hd, B_VDIM:B_VDIM + 1, :])
        gate = block_rows(gate_ref, qb, hd)
        outs = []
        for u in range(n_units):
            a = a_all[:, u * 2 * LANES:(u + 1) * 2 * LANES]
            o = a[:, :LANES] - lam * a[:, LANES:]
            ms = jnp.mean(o * o, axis=0, keepdims=True)
            y = (o * lax.rsqrt(ms + EPS) * sg) * (1.0 - lambda_init)
            yt = y.T[:nq, :]
            outs.append((yt * gate[u * nq:(u + 1) * nq, :].astype(f32)).astype(o_ref.dtype))
        o_ref[pl.ds(pl.multiple_of(qb * bq, bq), bq), cols(hd)] = jnp.concatenate(outs, axis=0)

    def block(qb, carry):
        for hd in heads:
            m = mxm_sc[hd]
            p = _pad_rows(jnp.exp2(sm_sc[hd] - m).astype(jnp.bfloat16), LANES)
            m_sc[hd] = m
            acc_sc[hd] = jnp.dot(with_ones(vmts[hd]), p, preferred_element_type=f32)
        for c in range(n_chunks):
            if c + 1 < n_chunks:
                for hd in heads:
                    scores(c + 1, hd)
            if c > 0:
                for hd in heads:
                    pv(c - 1, hd)
            for hd in heads:
                softmax(c, hd)
        for hd in heads:
            pv(n_chunks - 1, hd)
        for hd in heads:
            start_block(jnp.minimum(qb + 1, n_qblocks - 1), hd)
        for hd in heads:
            finish_block(qb, hd)
        return carry

    for hd in heads:
        start_block(0, hd)
    lax.fori_loop(0, n_qblocks, block, 0)


def _diff_attn(lams, sg, q, k, km, vt, vm, gate, *, bsz, seq, tm, q_block, tiles_per_chunk, meta_q,
               lambda_init):
    n_tiles = seq // tm
    tiles_per_chunk = math.gcd(tiles_per_chunk, n_tiles)
    ck = tiles_per_chunk * tm
    n_chunks = n_tiles // tiles_per_chunk
    if meta_q:
        nq, n_units, n_qblocks, n_heads = N_META, 1, 1, B_HEADS
    else:
        nq, n_units, n_qblocks, n_heads = LANES, q_block // LANES, seq // q_block, 1
    q_rows = nq * n_units * n_qblocks
    width = n_units * 2 * LANES
    hw = n_heads * LANES
    kernel = functools.partial(_diff_attn_kernel, n_chunks=n_chunks, tiles_per_chunk=tiles_per_chunk,
                               n_units=n_units, nq=nq, n_qblocks=n_qblocks, n_heads=n_heads,
                               lambda_init=lambda_init)
    lam_spec = pl.BlockSpec((1, HEAD_DIM), lambda b, h: (0, 0))
    bh = lambda b, h: (b, h)
    return pl.pallas_call(
        kernel,
        grid=(bsz, B_HEADS // n_heads),
        in_specs=[
            lam_spec, lam_spec, lam_spec, lam_spec,
            pl.BlockSpec((B_VDIM, 1), lambda b, h: (0, 0)),
            pl.BlockSpec((q_rows, hw), bh),
            pl.BlockSpec((seq, hw), bh),
            pl.BlockSpec((N_META, hw), bh),
            pl.BlockSpec((n_tiles, hw, tm), lambda b, h: (b, h, 0)),
            pl.BlockSpec((N_META, hw), bh),
            pl.BlockSpec((q_rows, hw), bh),
        ],
        out_specs=pl.BlockSpec((q_rows, hw), bh),
        out_shape=jax.ShapeDtypeStruct((bsz * q_rows, B_WIDTH), jnp.bfloat16),
        scratch_shapes=[
            pltpu.VMEM((n_heads, width, LANES), jnp.bfloat16),
            pltpu.VMEM((n_heads, 1, width), jnp.float32),
            pltpu.VMEM((n_heads, B_VDIM + SUM_ROWS, width), jnp.float32),
            pltpu.VMEM((n_heads, 2, ck, width), jnp.float32),
            pltpu.VMEM((n_heads, 2, 1, width), jnp.float32),
            pltpu.VMEM((n_heads, N_META, width), jnp.float32),
            pltpu.VMEM((n_heads, 1, width), jnp.float32),
            pltpu.VMEM((n_heads, 2, ck, width), jnp.bfloat16),
            pltpu.VMEM((n_heads, 2, 1, width), jnp.float32),
        ],
        compiler_params=_cparams(("parallel", "arbitrary")),
        name="diff_attn_meta" if meta_q else "diff_attn",
    )(*lams, sg, q, k, km, vt, vm, gate)


def _window_attn_kernel(sink_ref, q_ref, k_ref, ks_ref, km_ref, kms_ref, vt_ref, vm_ref, gate_ref, o_ref,
                        s_sc, mx_sc, p_sc, m_sc, *, nb, nq, blocks_per_step, meta_q):
    f32 = jnp.float32
    bf = jnp.bfloat16
    step = pl.program_id(1)
    n_keys = s_sc.shape[1]
    n_keys_padded = p_sc.shape[1]
    lane = lax.broadcasted_iota(jnp.int32, (1, LANES), 1)
    lo = lane < HEAD_DIM
    zero = jnp.zeros((LANES, LANES), bf)
    heads = (0, 2, 5, 7, 1, 3, 4, 6)

    km = km_ref[...]
    kms = kms_ref[...]
    _, vmt = _meta_kv(km_ref, vm_ref)
    key_i = lax.broadcasted_iota(jnp.int32, (LANES, LANES), 0)
    qry_i = lax.broadcasted_iota(jnp.int32, (LANES, LANES), 1)
    sink_row = jnp.concatenate([jnp.full((1, LANES), sink_ref[hd] * LOG2E, f32) for hd in heads], axis=1)
    for slot in range(2):
        p_sc[slot, n_keys:, :] = jnp.zeros((n_keys_padded - n_keys, p_sc.shape[2]), bf)

    def rows_of(ref, blk):
        return ref[pl.ds(pl.multiple_of(blk * BLOCK, BLOCK), BLOCK), :]

    def pieces(n):
        if meta_q:
            return ([k_ref[0:BLOCK, :], km], [ks_ref[0:BLOCK, :], kms], [vt_ref[0], vmt],
                    [(N_META + key_i - qry_i) <= WINDOW])
        prev = jnp.maximum(n - 1, 0)
        nxt = jnp.minimum(n + 1, nb - 1)
        return ([rows_of(k_ref, prev), rows_of(k_ref, n), rows_of(k_ref, nxt), km],
                [rows_of(ks_ref, prev), rows_of(ks_ref, n), rows_of(ks_ref, nxt), kms],
                [vt_ref[prev], vt_ref[n], vt_ref[nxt], vmt],
                [(key_i >= qry_i) & (n > 0), None, (key_i <= qry_i) & (n < nb - 1)])

    def scores(j):
        n = step * blocks_per_step + j
        q = _pad_rows(q_ref[j * nq:(j + 1) * nq, :], LANES)
        cols = [q[:, c * LANES:(c + 1) * LANES] for c in range(A_Q // LANES)]
        lo_of = [jnp.where(lo, c, zero) for c in cols]
        hi_of = [jnp.where(lo, zero, c) for c in cols]
        q_plain = jnp.concatenate([lo_of[0], lo_of[1], hi_of[2], hi_of[3]], axis=0)
        q_swap = jnp.concatenate([hi_of[0], hi_of[1], lo_of[2], lo_of[3]], axis=0)
        pk, pks, _, masks = pieces(n)
        s = jnp.concatenate(
            [lax.dot_general(jnp.concatenate(pk, axis=0), q_plain, _NT, preferred_element_type=f32),
             lax.dot_general(jnp.concatenate(pks, axis=0), q_swap, _NT, preferred_element_type=f32)],
            axis=1)
        cols = []
        for t in range(len(heads)):
            sh = s[:, t * LANES:(t + 1) * LANES]
            rows = [sh[r * LANES:(r + 1) * LANES, :] if mask is None
                    else jnp.where(mask, sh[r * LANES:(r + 1) * LANES, :], NEG_INF)
                    for r, mask in enumerate(masks)]
            rows.append(sh[len(masks) * LANES:, :])
            cols.append(jnp.concatenate(rows, axis=0))
        s = jnp.concatenate(cols, axis=1)
        s_sc[j % 2] = s
        mx_sc[j % 2] = jnp.max(s, axis=0, keepdims=True)

    def softmax(j):
        m = jnp.maximum(mx_sc[j % 2], sink_row)
        p_sc[j % 2, :n_keys, :] = jnp.exp2(s_sc[j % 2] - m).astype(bf)
        m_sc[j % 2] = m

    def pv(j):
        n = step * blocks_per_step + j
        _, _, pvt, _ = pieces(n)
        vt = jnp.concatenate(pvt, axis=1)
        vt = jnp.concatenate([vt, jnp.ones((SUM_ROWS, vt.shape[1]), bf)], axis=0)
        ot = jnp.dot(vt, p_sc[j % 2], preferred_element_type=f32)
        l = ot[2 * HEAD_DIM:2 * HEAD_DIM + 1, :] + jnp.exp2(sink_row - m_sc[j % 2])
        ot = ot[:2 * HEAD_DIM, :] * (1.0 / l)
        out_heads = [None] * A_HEADS
        for t, hd in enumerate(heads):
            g = hd // (A_HEADS // A_KV_HEADS)
            out_heads[hd] = ot[g * HEAD_DIM:(g + 1) * HEAD_DIM, t * LANES:(t + 1) * LANES]
        o = jnp.concatenate(out_heads, axis=0).T[:nq, :]
        o_ref[j * nq:(j + 1) * nq, :] = (o * gate_ref[j * nq:(j + 1) * nq, :].astype(f32)).astype(o_ref.dtype)

    scores(0)
    for j in range(blocks_per_step):
        if j > 0:
            pv(j - 1)
        softmax(j)
        if j + 1 < blocks_per_step:
            scores(j + 1)
    pv(blocks_per_step - 1)


def _window_attn(sink, q, k, ks, km, kms, vt, vm, gate, *, bsz, seq, blocks_per_step, meta_q):
    nb = seq // BLOCK
    if meta_q:
        nq, bps, n_steps, n_real_keys = N_META, 1, 1, BLOCK
        k_spec = pl.BlockSpec((BLOCK, A_KV), lambda b, n: (b * nb, 0))
        vt_spec = pl.BlockSpec((1, A_KV, LANES), lambda b, n: (b * nb, 0, 0))
    else:
        bps = math.gcd(blocks_per_step, nb)
        nq, n_steps, n_real_keys = BLOCK, nb // bps, 3 * BLOCK
        k_spec = pl.BlockSpec((seq, A_KV), lambda b, n: (b, 0))
        vt_spec = pl.BlockSpec((nb, A_KV, LANES), lambda b, n: (b, 0, 0))
    q_index = lambda b, n: (b * n_steps + n, 0)
    meta_spec = pl.BlockSpec((N_META, A_KV), lambda b, n: (b, 0))
    return pl.pallas_call(
        functools.partial(_window_attn_kernel, nb=nb, nq=nq, blocks_per_step=bps, meta_q=meta_q),
        grid=(bsz, n_steps),
        in_specs=[
            pl.BlockSpec(memory_space=pltpu.SMEM),
            pl.BlockSpec((bps * nq, A_Q), q_index),
            k_spec, k_spec, meta_spec, meta_spec, vt_spec, meta_spec,
            pl.BlockSpec((bps * nq, A_WIDTH), q_index),
        ],
        out_specs=pl.BlockSpec((bps * nq, A_WIDTH), q_index),
        out_shape=jax.ShapeDtypeStruct((bsz * n_steps * bps * nq, A_WIDTH), jnp.bfloat16),
        scratch_shapes=[
            pltpu.VMEM((2, n_real_keys + N_META, A_HEADS * LANES), jnp.float32),
            pltpu.VMEM((2, 1, A_HEADS * LANES), jnp.float32),
            pltpu.VMEM((2, n_real_keys + LANES, A_HEADS * LANES), jnp.bfloat16),
            pltpu.VMEM((2, 1, A_HEADS * LANES), jnp.float32),
        ],
        compiler_params=_cparams(("parallel", "arbitrary")),
        name="window_attn_meta" if meta_q else "window_attn",
    )(sink, q, k, ks, km, kms, vt, vm, gate)


def _rope_tables(pos):
    inv_freq = 1.0 / (ROPE_THETA ** (jnp.arange(0, HEAD_DIM, 2, dtype=jnp.float32) / HEAD_DIM))
    ang = pos.astype(jnp.float32)[:, None] * inv_freq[None, :]
    ang = jnp.concatenate([ang, ang, ang, ang], axis=-1)
    sign = jnp.where((jnp.arange(LANES) % HEAD_DIM) < HEAD_DIM // 2, -1.0, 1.0).astype(jnp.float32)
    return jnp.cos(ang), jnp.sin(ang) * sign[None, :]


def _encode_both(x_prompt, x_sample, meta_tokens, w_in, w_out, pre_norm_g, post_norm_g, sink_logits,
                 lambda_q1, lambda_k1, lambda_q2, lambda_k2, subln_g, *, tm, q_block):
    d = x_prompt.shape[-1]
    depth = w_in.shape[0]
    assert d == D_MODEL and tm % LANES == 0 and q_block % LANES == 0
    groups = []
    for x in (x_prompt, x_sample):
        bsz, seq, _ = x.shape
        assert seq % tm == 0 and seq % q_block == 0
        groups.append(dict(
            bsz=bsz, seq=seq,
            h_real=x.reshape(bsz * seq, d),
            h_meta=jnp.broadcast_to(meta_tokens[None], (bsz, N_META, d)).reshape(bsz * N_META, d),
            tab_real=_rope_tables(N_META + jnp.arange(seq)),
            tab_meta=_rope_tables(jnp.tile(jnp.arange(N_META), bsz)),
        ))

    for l in range(depth):
        lambda_init = 0.8 - 0.6 * math.exp(-0.3 * l)
        w_i = w_in[l].astype(jnp.bfloat16)
        w_o = w_out[l].astype(jnp.bfloat16)
        pre_g = pre_norm_g[l][None, :]
        post_g = post_norm_g[l][None, :]
        lams = (lambda_q1[l][None, :], lambda_k1[l][None, :], lambda_q2[l][None, :], lambda_k2[l][None, :])
        sg = subln_g[l][:, None]
        for grp in groups:
            bsz, seq = grp["bsz"], grp["seq"]
            real = _proj_call(grp["h_real"], pre_g, w_i, *grp["tab_real"], tm, True, grp.get("prev_real"))
            meta = _proj_call(grp["h_meta"], pre_g, w_i, *grp["tab_meta"], bsz * N_META, False,
                              grp.get("prev_meta"))
            if l > 0:
                grp["h_real"], grp["h_meta"] = real[0], meta[0]
                real, meta = real[1:], meta[1:]
            (aq, ak, aks, avt, ag, bq, bk, bvt, bg) = real
            (maq, mak, maks, mav, mag, mbq, mbk, mbv, mbg) = meta
            win = functools.partial(_window_attn, sink_logits[l], bsz=bsz, seq=seq,
                                    blocks_per_step=WINDOW_BLOCKS_PER_STEP)
            oa = win(aq, ak, aks, mak, maks, avt, mav, ag, meta_q=False)
            moa = win(maq, ak, aks, mak, maks, avt, mav, mag, meta_q=True)
            dif = functools.partial(_diff_attn, lams, sg, bsz=bsz, seq=seq, tm=tm, q_block=q_block,
                                    tiles_per_chunk=DIFF_TILES_PER_CHUNK, lambda_init=lambda_init)
            ob = dif(bq, bk, mbk, bvt, mbv, bg, meta_q=False)
            mob = dif(mbq, bk, mbk, bvt, mbv, mbg, meta_q=True)
            grp["prev_real"] = (oa, ob, w_o[:A_WIDTH], w_o[A_WIDTH:], post_g)
            grp["prev_meta"] = (moa, mob, w_o[:A_WIDTH], w_o[A_WIDTH:], post_g)

    return tuple(_out_proj(*g["prev_real"], g["h_real"], tm).reshape(g["bsz"], g["seq"], d) for g in groups)


def kernel(x_prompt, x_sample, meta_tokens, w_in, w_out, pre_norm_g, post_norm_g, sink_logits,
           lambda_q1, lambda_k1, lambda_q2, lambda_k2, subln_g):
    return _encode_both(x_prompt, x_sample, meta_tokens, w_in, w_out, pre_norm_g, post_norm_g, sink_logits,
                        lambda_q1, lambda_k1, lambda_q2, lambda_k2, subln_g,
                        tm=ROW_TILE, q_block=DIFF_Q_BLOCK)
```

```python
import functools
import math

import jax
import jax.numpy as jnp
from jax import lax
from jax.experimental import pallas as pl
from jax.experimental.pallas import tpu as pltpu

D_MODEL = 1024
N_META = 16
HEAD_DIM = 64
BLOCK = 128
WINDOW = 128
ROPE_THETA = 10000.0
EPS = 1e-6
NEG_INF = -1e30

A_WIDTH = 512
A_HEADS = 8
A_KV_HEADS = 2
B_WIDTH = 512
B_VDIM = 128
B_HEADS = 4
A_Q = A_HEADS * HEAD_DIM
A_KV = A_KV_HEADS * HEAD_DIM
B_QK = 2 * B_HEADS * HEAD_DIM
IN_WIDTH = A_Q + 2 * A_KV + A_WIDTH + 2 * B_QK + 2 * B_WIDTH

OFF_AQ = 0
OFF_AK = OFF_AQ + A_Q
OFF_AV = OFF_AK + A_KV
OFF_AG = OFF_AV + A_KV
OFF_BQ = OFF_AG + A_WIDTH
OFF_BK = OFF_BQ + B_QK
OFF_BV = OFF_BK + B_QK
OFF_BG = OFF_BV + B_WIDTH

LANES = 128
LOG2E = math.log2(math.e)
Q_SCALE = HEAD_DIM ** -0.5 * LOG2E

ROW_TILE = 512
PROJ_ROW_PARTS = 2
DIFF_Q_BLOCK = 512
DIFF_TILES_PER_CHUNK = 1
DIFF_BLOCKS_PER_ITER = 2
SUM_ROWS = 16
WINDOW_BLOCKS_PER_STEP = 16
VMEM_LIMIT = 56 * 1024 * 1024

_NT = (((1,), (1,)), ((), ()))


def _cparams(sem):
    return pltpu.CompilerParams(dimension_semantics=sem, vmem_limit_bytes=VMEM_LIMIT)


def _project_rows(x, row0, g_ref, w_ref, cos_ref, sin_ref,
                  aq_ref, ak_ref, aks_ref, av_ref, ag_ref, bq_ref, bk_ref, bv_ref, bg_ref, transposed_v):
    tm = x.shape[0]
    rows = slice(row0, row0 + tm)
    ms = jnp.mean(x * x, axis=-1, keepdims=True)
    hn = (x * lax.rsqrt(ms + EPS) * g_ref[...]).astype(jnp.bfloat16)
    cos = cos_ref[rows, :]
    sin = sin_ref[rows, :]
    lane = lax.broadcasted_iota(jnp.int32, (1, LANES), 1)
    first_half = (lane % HEAD_DIM) < (HEAD_DIM // 2)

    def proj(off, width):
        return jnp.dot(hn, w_ref[:, off:off + width], preferred_element_type=jnp.float32)

    def rope(p):
        fwd = pltpu.roll(p, LANES - HEAD_DIM // 2, 1)
        bwd = pltpu.roll(p, HEAD_DIM // 2, 1)
        return p * cos + jnp.where(first_half, fwd, bwd) * sin

    def silu(p):
        return p * (1.0 / (1.0 + jnp.exp(-p)))

    p = proj(OFF_AQ, A_Q)
    for j in range(A_Q // LANES):
        sl = slice(j * LANES, (j + 1) * LANES)
        aq_ref[rows, sl] = (rope(p[:, sl]) * Q_SCALE).astype(jnp.bfloat16)

    p = proj(OFF_AK, 2 * A_KV)
    k = rope(p[:, :A_KV])
    ak_ref[rows, :] = k.astype(jnp.bfloat16)
    aks_ref[rows, :] = pltpu.roll(k, HEAD_DIM, 1).astype(jnp.bfloat16)
    v = p[:, A_KV:]
    if transposed_v:
        vt = v.T
        for j in range(tm // LANES):
            av_ref[row0 // LANES + j] = vt[:, j * LANES:(j + 1) * LANES].astype(jnp.bfloat16)
    else:
        av_ref[rows, :] = v.astype(jnp.bfloat16)

    ag_ref[rows, :] = silu(proj(OFF_AG, A_WIDTH)).astype(jnp.bfloat16)

    p = proj(OFF_BQ, B_QK)
    for j in range(B_QK // LANES):
        sl = slice(j * LANES, (j + 1) * LANES)
        bq_ref[rows, sl] = (rope(p[:, sl]) * Q_SCALE).astype(jnp.bfloat16)

    p = proj(OFF_BK, B_QK)
    for j in range(B_QK // LANES):
        sl = slice(j * LANES, (j + 1) * LANES)
        bk_ref[rows, sl] = rope(p[:, sl]).astype(jnp.bfloat16)

    p = proj(OFF_BV, B_WIDTH)
    if transposed_v:
        bv_ref[0, :, rows] = p.T.astype(jnp.bfloat16)
    else:
        bv_ref[rows, :] = p.astype(jnp.bfloat16)

    bg_ref[rows, :] = silu(proj(OFF_BG, B_WIDTH)).astype(jnp.bfloat16)


def _row_parts(tm):
    part = tm // PROJ_ROW_PARTS if tm % (PROJ_ROW_PARTS * LANES) == 0 else tm
    return [slice(r, r + part) for r in range(0, tm, part)]


def _in_proj_kernel(h_ref, g_ref, w_ref, cos_ref, sin_ref, *out_refs, transposed_v):
    for rows in _row_parts(h_ref.shape[0]):
        _project_rows(h_ref[rows, :], rows.start, g_ref, w_ref, cos_ref, sin_ref, *out_refs, transposed_v)


def _mid_proj_kernel(oa_ref, ob_ref, wa_ref, wb_ref, post_g_ref, h_ref, g_ref, w_ref, cos_ref, sin_ref,
                     h_out_ref, *out_refs, transposed_v):
    parts = _row_parts(h_ref.shape[0])
    outs = [jnp.dot(oa_ref[rows, :], wa_ref[...], preferred_element_type=jnp.float32)
            + jnp.dot(ob_ref[rows, :], wb_ref[...], preferred_element_type=jnp.float32) for rows in parts]
    for rows, o in zip(parts, outs):
        ms = jnp.mean(o * o, axis=-1, keepdims=True)
        x = h_ref[rows, :] + o * lax.rsqrt(ms + EPS) * post_g_ref[...]
        h_out_ref[rows, :] = x
        _project_rows(x, rows.start, g_ref, w_ref, cos_ref, sin_ref, *out_refs, transposed_v)


def _proj_call(h, pre_g, w_bf16, cos_tab, sin_tab, tm, transposed_v, prev=None):
    rows = h.shape[0]
    nt = rows // tm
    n_tab = cos_tab.shape[0] // tm
    bf = jnp.bfloat16

    def row_spec(width):
        return pl.BlockSpec((tm, width), lambda t: (t, 0))

    def row_shape(width):
        return jax.ShapeDtypeStruct((rows, width), bf)

    if transposed_v:
        av_shape = jax.ShapeDtypeStruct((rows // LANES, A_KV, LANES), bf)
        av_spec = pl.BlockSpec((tm // LANES, A_KV, LANES), lambda t: (t, 0, 0))
        bv_shape = jax.ShapeDtypeStruct((nt, B_WIDTH, tm), bf)
        bv_spec = pl.BlockSpec((1, B_WIDTH, tm), lambda t: (t, 0, 0))
    else:
        av_shape, av_spec = row_shape(A_KV), row_spec(A_KV)
        bv_shape, bv_spec = row_shape(B_WIDTH), row_spec(B_WIDTH)
    out_shape = (row_shape(A_Q), row_shape(A_KV), row_shape(A_KV), av_shape, row_shape(A_WIDTH),
                 row_shape(B_QK), row_shape(B_QK), bv_shape, row_shape(B_WIDTH))
    out_specs = (row_spec(A_Q), row_spec(A_KV), row_spec(A_KV), av_spec, row_spec(A_WIDTH),
                 row_spec(B_QK), row_spec(B_QK), bv_spec, row_spec(B_WIDTH))
    tab_spec = pl.BlockSpec((tm, LANES), lambda t: (t % n_tab, 0))
    h_spec = pl.BlockSpec((tm, D_MODEL), lambda t: (t, 0))
    gain_spec = pl.BlockSpec((1, D_MODEL), lambda t: (0, 0))
    in_specs = [h_spec, gain_spec, pl.BlockSpec((D_MODEL, IN_WIDTH), lambda t: (0, 0)), tab_spec, tab_spec]
    args = (h, pre_g, w_bf16, cos_tab, sin_tab)
    if prev is None:
        kernel, name = _in_proj_kernel, "in_proj"
    else:
        kernel, name = _mid_proj_kernel, "mid_proj"
        oa, ob, wa, wb, post_g = prev
        in_specs = [row_spec(A_WIDTH), row_spec(B_WIDTH),
                    pl.BlockSpec((A_WIDTH, D_MODEL), lambda t: (0, 0)),
                    pl.BlockSpec((B_WIDTH, D_MODEL), lambda t: (0, 0)), gain_spec] + in_specs
        args = (oa, ob, wa, wb, post_g) + args
        out_shape = (jax.ShapeDtypeStruct(h.shape, h.dtype),) + out_shape
        out_specs = (h_spec,) + out_specs
    return pl.pallas_call(
        functools.partial(kernel, transposed_v=transposed_v),
        grid=(nt,), in_specs=in_specs, out_specs=out_specs, out_shape=out_shape,
        compiler_params=_cparams(("parallel",)), name=name if transposed_v else name + "_meta",
    )(*args)


def _out_proj_kernel(oa_ref, ob_ref, wa_ref, wb_ref, g_ref, h_ref, out_ref):
    o = jnp.dot(oa_ref[...], wa_ref[...], preferred_element_type=jnp.float32)
    o = o + jnp.dot(ob_ref[...], wb_ref[...], preferred_element_type=jnp.float32)
    ms = jnp.mean(o * o, axis=-1, keepdims=True)
    out_ref[...] = h_ref[...] + o * lax.rsqrt(ms + EPS) * g_ref[...]


def _out_proj(oa, ob, wa, wb, post_g, h, tm):
    rows = h.shape[0]
    return pl.pallas_call(
        _out_proj_kernel,
        grid=(rows // tm,),
        in_specs=[
            pl.BlockSpec((tm, A_WIDTH), lambda t: (t, 0)),
            pl.BlockSpec((tm, B_WIDTH), lambda t: (t, 0)),
            pl.BlockSpec((A_WIDTH, D_MODEL), lambda t: (0, 0)),
            pl.BlockSpec((B_WIDTH, D_MODEL), lambda t: (0, 0)),
            pl.BlockSpec((1, D_MODEL), lambda t: (0, 0)),
            pl.BlockSpec((tm, D_MODEL), lambda t: (t, 0)),
        ],
        out_specs=pl.BlockSpec((tm, D_MODEL), lambda t: (t, 0)),
        out_shape=jax.ShapeDtypeStruct(h.shape, h.dtype),
        compiler_params=_cparams(("parallel",)), name="out_proj",
    )(oa, ob, wa, wb, post_g, h)


def _pad_rows(x, rows):
    if x.shape[0] == rows:
        return x
    return jnp.concatenate([x, jnp.zeros((rows - x.shape[0], x.shape[1]), x.dtype)], axis=0)


def _meta_kv(km_ref, vm_ref):
    kmp = _pad_rows(km_ref[...], LANES)
    vmt = _pad_rows(vm_ref[...].astype(jnp.float32), LANES).T.astype(jnp.bfloat16)
    return kmp, vmt


def _diff_attn_kernel(lq1_ref, lk1_ref, lq2_ref, lk2_ref, sg_ref,
                      q_ref, k_ref, km_ref, vt_ref, vm_ref, gate_ref, o_ref,
                      qcat_sc, m_sc, acc_sc, s_sc, mx_sc, sm_sc, mxm_sc, p_sc, alpha_sc,
                      *, n_chunks, tiles_per_chunk, n_units, nq, n_qblocks, n_heads, lambda_init):
    f32 = jnp.float32
    bq = n_units * nq
    ck = tiles_per_chunk * vt_ref.shape[2]
    lam = (jnp.exp(jnp.sum(lq1_ref[...] * lk1_ref[...], axis=-1, keepdims=True))
           - jnp.exp(jnp.sum(lq2_ref[...] * lk2_ref[...], axis=-1, keepdims=True)) + lambda_init)
    sg = sg_ref[...]
    lane = lax.broadcasted_iota(jnp.int32, (1, LANES), 1)
    comp0 = lane < HEAD_DIM
    heads = range(n_heads)

    def cols(hd):
        return slice(hd * LANES, (hd + 1) * LANES)

    kms = [km_ref[:, cols(hd)] for hd in heads]
    vmts = [_pad_rows(vm_ref[:, cols(hd)].astype(f32), LANES).T.astype(jnp.bfloat16) for hd in heads]

    def with_ones(vt):
        return jnp.concatenate([vt, jnp.ones((SUM_ROWS, vt.shape[1]), vt.dtype)], axis=0)

    def block_rows(ref, qb, hd):
        return ref[pl.ds(pl.multiple_of(qb * bq, bq), bq), cols(hd)]

    def start_block(qb, hd):
        qrows = block_rows(q_ref, qb, hd)
        qcats = []
        for u in range(n_units):
            q = _pad_rows(qrows[u * nq:(u + 1) * nq, :], LANES)
            zero = jnp.zeros_like(q)
            qcats += [jnp.where(comp0, q, zero), jnp.where(comp0, zero, q)]
        qcat = jnp.concatenate(qcats, axis=0)
        qcat_sc[hd] = qcat
        s = lax.dot_general(kms[hd], qcat, _NT, preferred_element_type=f32)
        sm_sc[hd] = s
        mxm_sc[hd] = jnp.max(s, axis=0, keepdims=True)
        scores(0, hd)

    def scores(c, hd):
        s = lax.dot_general(k_ref[c * ck:(c + 1) * ck, cols(hd)], qcat_sc[hd], _NT,
                            preferred_element_type=f32)
        s_sc[hd, c % 2] = s
        mx_sc[hd, c % 2] = jnp.max(s, axis=0, keepdims=True)

    def softmax(c, hd):
        m_old = m_sc[hd]
        m_new = jnp.maximum(m_old, mx_sc[hd, c % 2])
        alpha = jnp.exp2(m_old - m_new)
        p_sc[hd, c % 2] = jnp.exp2(s_sc[hd, c % 2] - m_new).astype(jnp.bfloat16)
        alpha_sc[hd, c % 2] = alpha
        m_sc[hd] = m_new

    def pv(c, hd):
        vt = jnp.concatenate([vt_ref[c * tiles_per_chunk + i, cols(hd), :] for i in range(tiles_per_chunk)],
                             axis=1)
        acc_sc[hd] = alpha_sc[hd, c % 2] * acc_sc[hd] + jnp.dot(with_ones(vt), p_sc[hd, c % 2],
                                                                preferred_element_type=f32)

    def finish_block(qb, hd):
        a_all = acc_sc[hd, :B_VDIM, :] * (1.0 / acc_sc[hd, B_VDIM:B_VDIM + 1, :])
        gate = block_rows(gate_ref, qb, hd)
        outs = []
        for u in range(n_units):
            a = a_all[:, u * 2 * LANES:(u + 1) * 2 * LANES]
            o = a[:, :LANES] - lam * a[:, LANES:]
            ms = jnp.mean(o * o, axis=0, keepdims=True)
            y = (o * lax.rsqrt(ms + EPS) * sg) * (1.0 - lambda_init)
            yt = y.T[:nq, :]
            outs.append((yt * gate[u * nq:(u + 1) * nq, :].astype(f32)).astype(o_ref.dtype))
        o_ref[pl.ds(pl.multiple_of(qb * bq, bq), bq), cols(hd)] = jnp.concatenate(outs, axis=0)

    def block(qb, carry):
        for hd in heads:
            m = mxm_sc[hd]
            p = _pad_rows(jnp.exp2(sm_sc[hd] - m).astype(jnp.bfloat16), LANES)
            m_sc[hd] = m
            acc_sc[hd] = jnp.dot(with_ones(vmts[hd]), p, preferred_element_type=f32)
        for c in range(n_chunks):
            if c + 1 < n_chunks:
                for hd in heads:
                    scores(c + 1, hd)
            if c > 0:
                for hd in heads:
                    pv(c - 1, hd)
            for hd in heads:
                softmax(c, hd)
        for hd in heads:
            pv(n_chunks - 1, hd)
        for hd in heads:
            start_block(jnp.minimum(qb + 1, n_qblocks - 1), hd)
        for hd in heads:
            finish_block(qb, hd)
        return carry

    def block_group(i, carry):
        for j in range(blocks_per_iter):
            block(i * blocks_per_iter + j, carry)
        return carry

    blocks_per_iter = math.gcd(DIFF_BLOCKS_PER_ITER, n_qblocks)
    for hd in heads:
        start_block(0, hd)
    lax.fori_loop(0, n_qblocks // blocks_per_iter, block_group, 0)


def _diff_attn(lams, sg, q, k, km, vt, vm, gate, *, bsz, seq, tm, q_block, tiles_per_chunk, meta_q,
               lambda_init):
    n_tiles = seq // tm
    tiles_per_chunk = math.gcd(tiles_per_chunk, n_tiles)
    ck = tiles_per_chunk * tm
    n_chunks = n_tiles // tiles_per_chunk
    if meta_q:
        nq, n_units, n_qblocks, n_heads = N_META, 1, 1, B_HEADS
    else:
        nq, n_units, n_qblocks, n_heads = LANES, q_block // LANES, seq // q_block, 1
    q_rows = nq * n_units * n_qblocks
    width = n_units * 2 * LANES
    hw = n_heads * LANES
    kernel = functools.partial(_diff_attn_kernel, n_chunks=n_chunks, tiles_per_chunk=tiles_per_chunk,
                               n_units=n_units, nq=nq, n_qblocks=n_qblocks, n_heads=n_heads,
                               lambda_init=lambda_init)
    lam_spec = pl.BlockSpec((1, HEAD_DIM), lambda b, h: (0, 0))
    bh = lambda b, h: (b, h)
    return pl.pallas_call(
        kernel,
        grid=(bsz, B_HEADS // n_heads),
        in_specs=[
            lam_spec, lam_spec, lam_spec, lam_spec,
            pl.BlockSpec((B_VDIM, 1), lambda b, h: (0, 0)),
            pl.BlockSpec((q_rows, hw), bh),
            pl.BlockSpec((seq, hw), bh),
            pl.BlockSpec((N_META, hw), bh),
            pl.BlockSpec((n_tiles, hw, tm), lambda b, h: (b, h, 0)),
            pl.BlockSpec((N_META, hw), bh),
            pl.BlockSpec((q_rows, hw), bh),
        ],
        out_specs=pl.BlockSpec((q_rows, hw), bh),
        out_shape=jax.ShapeDtypeStruct((bsz * q_rows, B_WIDTH), jnp.bfloat16),
        scratch_shapes=[
            pltpu.VMEM((n_heads, width, LANES), jnp.bfloat16),
            pltpu.VMEM((n_heads, 1, width), jnp.float32),
            pltpu.VMEM((n_heads, B_VDIM + SUM_ROWS, width), jnp.float32),
            pltpu.VMEM((n_heads, 2, ck, width), jnp.float32),
            pltpu.VMEM((n_heads, 2, 1, width), jnp.float32),
            pltpu.VMEM((n_heads, N_META, width), jnp.float32),
            pltpu.VMEM((n_heads, 1, width), jnp.float32),
            pltpu.VMEM((n_heads, 2, ck, width), jnp.bfloat16),
            pltpu.VMEM((n_heads, 2, 1, width), jnp.float32),
        ],
        compiler_params=_cparams(("parallel", "arbitrary")),
        name="diff_attn_meta" if meta_q else "diff_attn",
    )(*lams, sg, q, k, km, vt, vm, gate)


def _window_attn_kernel(sink_ref, q_ref, k_ref, ks_ref, km_ref, kms_ref, vt_ref, vm_ref, gate_ref, o_ref,
                        s_sc, mx_sc, p_sc, m_sc, *, nb, nq, blocks_per_step, meta_q):
    f32 = jnp.float32
    bf = jnp.bfloat16
    step = pl.program_id(1)
    n_keys = s_sc.shape[1]
    n_keys_padded = p_sc.shape[1]
    lane = lax.broadcasted_iota(jnp.int32, (1, LANES), 1)
    lo = lane < HEAD_DIM
    zero = jnp.zeros((LANES, LANES), bf)
    heads = (0, 2, 5, 7, 1, 3, 4, 6)

    km = km_ref[...]
    kms = kms_ref[...]
    _, vmt = _meta_kv(km_ref, vm_ref)
    key_i = lax.broadcasted_iota(jnp.int32, (LANES, LANES), 0)
    qry_i = lax.broadcasted_iota(jnp.int32, (LANES, LANES), 1)
    sink_row = jnp.concatenate([jnp.full((1, LANES), sink_ref[hd] * LOG2E, f32) for hd in heads], axis=1)
    for slot in range(2):
        p_sc[slot, n_keys:, :] = jnp.zeros((n_keys_padded - n_keys, p_sc.shape[2]), bf)

    def rows_of(ref, blk):
        return ref[pl.ds(pl.multiple_of(blk * BLOCK, BLOCK), BLOCK), :]

    def pieces(n):
        if meta_q:
            return ([k_ref[0:BLOCK, :], km], [ks_ref[0:BLOCK, :], kms], [vt_ref[0], vmt],
                    [(N_META + key_i - qry_i) <= WINDOW])
        prev = jnp.maximum(n - 1, 0)
        nxt = jnp.minimum(n + 1, nb - 1)
        return ([rows_of(k_ref, prev), rows_of(k_ref, n), rows_of(k_ref, nxt), km],
                [rows_of(ks_ref, prev), rows_of(ks_ref, n), rows_of(ks_ref, nxt), kms],
                [vt_ref[prev], vt_ref[n], vt_ref[nxt], vmt],
                [(key_i >= qry_i) & (n > 0), None, (key_i <= qry_i) & (n < nb - 1)])

    def scores(j):
        n = step * blocks_per_step + j
        q = _pad_rows(q_ref[j * nq:(j + 1) * nq, :], LANES)
        cols = [q[:, c * LANES:(c + 1) * LANES] for c in range(A_Q // LANES)]
        lo_of = [jnp.where(lo, c, zero) for c in cols]
        hi_of = [jnp.where(lo, zero, c) for c in cols]
        q_plain = jnp.concatenate([lo_of[0], lo_of[1], hi_of[2], hi_of[3]], axis=0)
        q_swap = jnp.concatenate([hi_of[0], hi_of[1], lo_of[2], lo_of[3]], axis=0)
        pk, pks, _, masks = pieces(n)
        s = jnp.concatenate(
            [lax.dot_general(jnp.concatenate(pk, axis=0), q_plain, _NT, preferred_element_type=f32),
             lax.dot_general(jnp.concatenate(pks, axis=0), q_swap, _NT, preferred_element_type=f32)],
            axis=1)
        cols = []
        for t in range(len(heads)):
            sh = s[:, t * LANES:(t + 1) * LANES]
            rows = [sh[r * LANES:(r + 1) * LANES, :] if mask is None
                    else jnp.where(mask, sh[r * LANES:(r + 1) * LANES, :], NEG_INF)
                    for r, mask in enumerate(masks)]
            rows.append(sh[len(masks) * LANES:, :])
            cols.append(jnp.concatenate(rows, axis=0))
        s = jnp.concatenate(cols, axis=1)
        s_sc[j % 2] = s
        mx_sc[j % 2] = jnp.max(s, axis=0, keepdims=True)

    def softmax(j):
        m = jnp.maximum(mx_sc[j % 2], sink_row)
        p_sc[j % 2, :n_keys, :] = jnp.exp2(s_sc[j % 2] - m).astype(bf)
        m_sc[j % 2] = m

    def pv(j):
        n = step * blocks_per_step + j
        _, _, pvt, _ = pieces(n)
        vt = jnp.concatenate(pvt, axis=1)
        vt = jnp.concatenate([vt, jnp.ones((SUM_ROWS, vt.shape[1]), bf)], axis=0)
        ot = jnp.dot(vt, p_sc[j % 2], preferred_element_type=f32)
        l = ot[2 * HEAD_DIM:2 * HEAD_DIM + 1, :] + jnp.exp2(sink_row - m_sc[j % 2])
        ot = ot[:2 * HEAD_DIM, :] * (1.0 / l)
        out_heads = [None] * A_HEADS
        for t, hd in enumerate(heads):
            g = hd // (A_HEADS // A_KV_HEADS)
            out_heads[hd] = ot[g * HEAD_DIM:(g + 1) * HEAD_DIM, t * LANES:(t + 1) * LANES]
        o = jnp.concatenate(out_heads, axis=0).T[:nq, :]
        o_ref[j * nq:(j + 1) * nq, :] = (o * gate_ref[j * nq:(j + 1) * nq, :].astype(f32)).astype(o_ref.dtype)

    scores(0)
    for j in range(blocks_per_step):
        if j > 0:
            pv(j - 1)
        softmax(j)
        if j + 1 < blocks_per_step:
            scores(j + 1)
    pv(blocks_per_step - 1)


def _window_attn(sink, q, k, ks, km, kms, vt, vm, gate, *, bsz, seq, blocks_per_step, meta_q):
    nb = seq // BLOCK
    if meta_q:
        nq, bps, n_steps, n_real_keys = N_META, 1, 1, BLOCK
        k_spec = pl.BlockSpec((BLOCK, A_KV), lambda b, n: (b * nb, 0))
        vt_spec = pl.BlockSpec((1, A_KV, LANES), lambda b, n: (b * nb, 0, 0))
    else:
        bps = math.gcd(blocks_per_step, nb)
        nq, n_steps, n_real_keys = BLOCK, nb // bps, 3 * BLOCK
        k_spec = pl.BlockSpec((seq, A_KV), lambda b, n: (b, 0))
        vt_spec = pl.BlockSpec((nb, A_KV, LANES), lambda b, n: (b, 0, 0))
    q_index = lambda b, n: (b * n_steps + n, 0)
    meta_spec = pl.BlockSpec((N_META, A_KV), lambda b, n: (b, 0))
    return pl.pallas_call(
        functools.partial(_window_attn_kernel, nb=nb, nq=nq, blocks_per_step=bps, meta_q=meta_q),
        grid=(bsz, n_steps),
        in_specs=[
            pl.BlockSpec(memory_space=pltpu.SMEM),
            pl.BlockSpec((bps * nq, A_Q), q_index),
            k_spec, k_spec, meta_spec, meta_spec, vt_spec, meta_spec,
            pl.BlockSpec((bps * nq, A_WIDTH), q_index),
        ],
        out_specs=pl.BlockSpec((bps * nq, A_WIDTH), q_index),
        out_shape=jax.ShapeDtypeStruct((bsz * n_steps * bps * nq, A_WIDTH), jnp.bfloat16),
        scratch_shapes=[
            pltpu.VMEM((2, n_real_keys + N_META, A_HEADS * LANES), jnp.float32),
            pltpu.VMEM((2, 1, A_HEADS * LANES), jnp.float32),
            pltpu.VMEM((2, n_real_keys + LANES, A_HEADS * LANES), jnp.bfloat16),
            pltpu.VMEM((2, 1, A_HEADS * LANES), jnp.float32),
        ],
        compiler_params=_cparams(("parallel", "arbitrary")),
        name="window_attn_meta" if meta_q else "window_attn",
    )(sink, q, k, ks, km, kms, vt, vm, gate)


def _rope_tables(pos):
    inv_freq = 1.0 / (ROPE_THETA ** (jnp.arange(0, HEAD_DIM, 2, dtype=jnp.float32) / HEAD_DIM))
    ang = pos.astype(jnp.float32)[:, None] * inv_freq[None, :]
    ang = jnp.concatenate([ang, ang, ang, ang], axis=-1)
    sign = jnp.where((jnp.arange(LANES) % HEAD_DIM) < HEAD_DIM // 2, -1.0, 1.0).astype(jnp.float32)
    return jnp.cos(ang), jnp.sin(ang) * sign[None, :]


def _encode_both(x_prompt, x_sample, meta_tokens, w_in, w_out, pre_norm_g, post_norm_g, sink_logits,
                 lambda_q1, lambda_k1, lambda_q2, lambda_k2, subln_g, *, tm, q_block):
    d = x_prompt.shape[-1]
    depth = w_in.shape[0]
    assert d == D_MODEL and tm % LANES == 0 and q_block % LANES == 0
    groups = []
    for x in (x_prompt, x_sample):
        bsz, seq, _ = x.shape
        assert seq % tm == 0 and seq % q_block == 0
        groups.append(dict(
            bsz=bsz, seq=seq,
            h_real=x.reshape(bsz * seq, d),
            h_meta=jnp.broadcast_to(meta_tokens[None], (bsz, N_META, d)).reshape(bsz * N_META, d),
            tab_real=_rope_tables(N_META + jnp.arange(seq)),
            tab_meta=_rope_tables(jnp.tile(jnp.arange(N_META), bsz)),
        ))

    for l in range(depth):
        lambda_init = 0.8 - 0.6 * math.exp(-0.3 * l)
        w_i = w_in[l].astype(jnp.bfloat16)
        w_o = w_out[l].astype(jnp.bfloat16)
        pre_g = pre_norm_g[l][None, :]
        post_g = post_norm_g[l][None, :]
        lams = (lambda_q1[l][None, :], lambda_k1[l][None, :], lambda_q2[l][None, :], lambda_k2[l][None, :])
        sg = subln_g[l][:, None]
        for grp in groups:
            bsz, seq = grp["bsz"], grp["seq"]
            real = _proj_call(grp["h_real"], pre_g, w_i, *grp["tab_real"], tm, True, grp.get("prev_real"))
            meta = _proj_call(grp["h_meta"], pre_g, w_i, *grp["tab_meta"], bsz * N_META, False,
                              grp.get("prev_meta"))
            if l > 0:
                grp["h_real"], grp["h_meta"] = real[0], meta[0]
                real, meta = real[1:], meta[1:]
            (aq, ak, aks, avt, ag, bq, bk, bvt, bg) = real
            (maq, mak, maks, mav, mag, mbq, mbk, mbv, mbg) = meta
            win = functools.partial(_window_attn, sink_logits[l], bsz=bsz, seq=seq,
                                    blocks_per_step=WINDOW_BLOCKS_PER_STEP)
            oa = win(aq, ak, aks, mak, maks, avt, mav, ag, meta_q=False)
            moa = win(maq, ak, aks, mak, maks, avt, mav, mag, meta_q=True)
            dif = functools.partial(_diff_attn, lams, sg, bsz=bsz, seq=seq, tm=tm, q_block=q_block,
                                    tiles_per_chunk=DIFF_TILES_PER_CHUNK, lambda_init=lambda_init)
            ob = dif(bq, bk, mbk, bvt, mbv, bg, meta_q=False)
            mob = dif(mbq, bk, mbk, bvt, mbv, mbg, meta_q=True)
            grp["prev_real"] = (oa, ob, w_o[:A_WIDTH], w_o[A_WIDTH:], post_g)
            grp["prev_meta"] = (moa, mob, w_o[:A_WIDTH], w_o[A_WIDTH:], post_g)

    return tuple(_out_proj(*g["prev_real"], g["h_real"], tm).reshape(g["bsz"], g["seq"], d) for g in groups)


def kernel(x_prompt, x_sample, meta_tokens, w_in, w_out, pre_norm_g, post_norm_g, sink_logits,
           lambda_q1, lambda_k1, lambda_q2, lambda_k2, subln_g):
    return _encode_both(x_prompt, x_sample, meta_tokens, w_in, w_out, pre_norm_g, post_norm_g, sink_logits,
                        lambda_q1, lambda_k1, lambda_q2, lambda_k2, subln_g,
                        tm=ROW_TILE, q_block=DIFF_Q_BLOCK)
```

```python
import functools
import math

import jax
import jax.numpy as jnp
from jax import lax
from jax.experimental import pallas as pl
from jax.experimental.pallas import tpu as pltpu

D_MODEL = 1024
N_META = 16
HEAD_DIM = 64
BLOCK = 128
WINDOW = 128
ROPE_THETA = 10000.0
EPS = 1e-6
NEG_INF = -1e30

A_WIDTH = 512
A_HEADS = 8
A_KV_HEADS = 2
B_WIDTH = 512
B_VDIM = 128
B_HEADS = 4
A_Q = A_HEADS * HEAD_DIM
A_KV = A_KV_HEADS * HEAD_DIM
B_QK = 2 * B_HEADS * HEAD_DIM
IN_WIDTH = A_Q + 2 * A_KV + A_WIDTH + 2 * B_QK + 2 * B_WIDTH

OFF_AQ = 0
OFF_AK = OFF_AQ + A_Q
OFF_AV = OFF_AK + A_KV
OFF_AG = OFF_AV + A_KV
OFF_BQ = OFF_AG + A_WIDTH
OFF_BK = OFF_BQ + B_QK
OFF_BV = OFF_BK + B_QK
OFF_BG = OFF_BV + B_WIDTH

LANES = 128
LOG2E = math.log2(math.e)
Q_SCALE = HEAD_DIM ** -0.5 * LOG2E

ROW_TILE = 512
PROJ_ROW_PARTS = 2
DIFF_Q_BLOCK = 512
DIFF_TILES_PER_CHUNK = 1
DIFF_BLOCKS_PER_ITER = 4
SUM_ROWS = 16
WINDOW_BLOCKS_PER_STEP = 32
VMEM_LIMIT = 56 * 1024 * 1024

_NT = (((1,), (1,)), ((), ()))


def _cparams(sem):
    return pltpu.CompilerParams(dimension_semantics=sem, vmem_limit_bytes=VMEM_LIMIT)


def _project_rows(x, row0, g_ref, w_ref, cos_ref, sin_ref,
                  aq_ref, ak_ref, aks_ref, av_ref, ag_ref, bq_ref, bk_ref, bv_ref, bg_ref, transposed_v):
    tm = x.shape[0]
    rows = slice(row0, row0 + tm)
    ms = jnp.mean(x * x, axis=-1, keepdims=True)
    hn = (x * lax.rsqrt(ms + EPS) * g_ref[...]).astype(jnp.bfloat16)
    cos = cos_ref[rows, :]
    sin = sin_ref[rows, :]
    lane = lax.broadcasted_iota(jnp.int32, (1, LANES), 1)
    first_half = (lane % HEAD_DIM) < (HEAD_DIM // 2)

    def proj(off, width):
        return jnp.dot(hn, w_ref[:, off:off + width], preferred_element_type=jnp.float32)

    def rope(p):
        fwd = pltpu.roll(p, LANES - HEAD_DIM // 2, 1)
        bwd = pltpu.roll(p, HEAD_DIM // 2, 1)
        return p * cos + jnp.where(first_half, fwd, bwd) * sin

    def silu(p):
        return p * (1.0 / (1.0 + jnp.exp(-p)))

    p = proj(OFF_AQ, A_Q)
    for j in range(A_Q // LANES):
        sl = slice(j * LANES, (j + 1) * LANES)
        aq_ref[rows, sl] = (rope(p[:, sl]) * Q_SCALE).astype(jnp.bfloat16)

    p = proj(OFF_AK, 2 * A_KV)
    k = rope(p[:, :A_KV])
    ak_ref[rows, :] = k.astype(jnp.bfloat16)
    aks_ref[rows, :] = pltpu.roll(k, HEAD_DIM, 1).astype(jnp.bfloat16)
    v = p[:, A_KV:]
    if transposed_v:
        vt = v.T
        for j in range(tm // LANES):
            av_ref[row0 // LANES + j] = vt[:, j * LANES:(j + 1) * LANES].astype(jnp.bfloat16)
    else:
        av_ref[rows, :] = v.astype(jnp.bfloat16)

    ag_ref[rows, :] = silu(proj(OFF_AG, A_WIDTH)).astype(jnp.bfloat16)

    p = proj(OFF_BQ, B_QK)
    for j in range(B_QK // LANES):
        sl = slice(j * LANES, (j + 1) * LANES)
        bq_ref[rows, sl] = (rope(p[:, sl]) * Q_SCALE).astype(jnp.bfloat16)

    p = proj(OFF_BK, B_QK)
    for j in range(B_QK // LANES):
        sl = slice(j * LANES, (j + 1) * LANES)
        bk_ref[rows, sl] = rope(p[:, sl]).astype(jnp.bfloat16)

    p = proj(OFF_BV, B_WIDTH)
    if transposed_v:
        bv_ref[0, :, rows] = p.T.astype(jnp.bfloat16)
    else:
        bv_ref[rows, :] = p.astype(jnp.bfloat16)

    bg_ref[rows, :] = silu(proj(OFF_BG, B_WIDTH)).astype(jnp.bfloat16)


def _row_parts(tm):
    part = tm // PROJ_ROW_PARTS if tm % (PROJ_ROW_PARTS * LANES) == 0 else tm
    return [slice(r, r + part) for r in range(0, tm, part)]


def _in_proj_kernel(h_ref, g_ref, w_ref, cos_ref, sin_ref, *out_refs, transposed_v):
    for rows in _row_parts(h_ref.shape[0]):
        _project_rows(h_ref[rows, :], rows.start, g_ref, w_ref, cos_ref, sin_ref, *out_refs, transposed_v)


def _mid_proj_kernel(oa_ref, ob_ref, wa_ref, wb_ref, post_g_ref, h_ref, g_ref, w_ref, cos_ref, sin_ref,
                     h_out_ref, *out_refs, transposed_v):
    parts = _row_parts(h_ref.shape[0])
    outs = [jnp.dot(oa_ref[rows, :], wa_ref[...], preferred_element_type=jnp.float32)
            + jnp.dot(ob_ref[rows, :], wb_ref[...], preferred_element_type=jnp.float32) for rows in parts]
    for rows, o in zip(parts, outs):
        ms = jnp.mean(o * o, axis=-1, keepdims=True)
        x = h_ref[rows, :] + o * lax.rsqrt(ms + EPS) * post_g_ref[...]
        h_out_ref[rows, :] = x
        _project_rows(x, rows.start, g_ref, w_ref, cos_ref, sin_ref, *out_refs, transposed_v)


def _proj_call(h, pre_g, w_bf16, cos_tab, sin_tab, tm, transposed_v, prev=None):
    rows = h.shape[0]
    nt = rows // tm
    n_tab = cos_tab.shape[0] // tm
    bf = jnp.bfloat16

    def row_spec(width):
        return pl.BlockSpec((tm, width), lambda t: (t, 0))

    def row_shape(width):
        return jax.ShapeDtypeStruct((rows, width), bf)

    if transposed_v:
        av_shape = jax.ShapeDtypeStruct((rows // LANES, A_KV, LANES), bf)
        av_spec = pl.BlockSpec((tm // LANES, A_KV, LANES), lambda t: (t, 0, 0))
        bv_shape = jax.ShapeDtypeStruct((nt, B_WIDTH, tm), bf)
        bv_spec = pl.BlockSpec((1, B_WIDTH, tm), lambda t: (t, 0, 0))
    else:
        av_shape, av_spec = row_shape(A_KV), row_spec(A_KV)
        bv_shape, bv_spec = row_shape(B_WIDTH), row_spec(B_WIDTH)
    out_shape = (row_shape(A_Q), row_shape(A_KV), row_shape(A_KV), av_shape, row_shape(A_WIDTH),
                 row_shape(B_QK), row_shape(B_QK), bv_shape, row_shape(B_WIDTH))
    out_specs = (row_spec(A_Q), row_spec(A_KV), row_spec(A_KV), av_spec, row_spec(A_WIDTH),
                 row_spec(B_QK), row_spec(B_QK), bv_spec, row_spec(B_WIDTH))
    tab_spec = pl.BlockSpec((tm, LANES), lambda t: (t % n_tab, 0))
    h_spec = pl.BlockSpec((tm, D_MODEL), lambda t: (t, 0))
    gain_spec = pl.BlockSpec((1, D_MODEL), lambda t: (0, 0))
    in_specs = [h_spec, gain_spec, pl.BlockSpec((D_MODEL, IN_WIDTH), lambda t: (0, 0)), tab_spec, tab_spec]
    args = (h, pre_g, w_bf16, cos_tab, sin_tab)
    if prev is None:
        kernel, name = _in_proj_kernel, "in_proj"
    else:
        kernel, name = _mid_proj_kernel, "mid_proj"
        oa, ob, wa, wb, post_g = prev
        in_specs = [row_spec(A_WIDTH), row_spec(B_WIDTH),
                    pl.BlockSpec((A_WIDTH, D_MODEL), lambda t: (0, 0)),
                    pl.BlockSpec((B_WIDTH, D_MODEL), lambda t: (0, 0)), gain_spec] + in_specs
        args = (oa, ob, wa, wb, post_g) + args
        out_shape = (jax.ShapeDtypeStruct(h.shape, h.dtype),) + out_shape
        out_specs = (h_spec,) + out_specs
    return pl.pallas_call(
        functools.partial(kernel, transposed_v=transposed_v),
        grid=(nt,), in_specs=in_specs, out_specs=out_specs, out_shape=out_shape,
        compiler_params=_cparams(("parallel",)), name=name if transposed_v else name + "_meta",
    )(*args)


def _out_proj_kernel(oa_ref, ob_ref, wa_ref, wb_ref, g_ref, h_ref, out_ref):
    o = jnp.dot(oa_ref[...], wa_ref[...], preferred_element_type=jnp.float32)
    o = o + jnp.dot(ob_ref[...], wb_ref[...], preferred_element_type=jnp.float32)
    ms = jnp.mean(o * o, axis=-1, keepdims=True)
    out_ref[...] = h_ref[...] + o * lax.rsqrt(ms + EPS) * g_ref[...]


def _out_proj(oa, ob, wa, wb, post_g, h, tm):
    rows = h.shape[0]
    return pl.pallas_call(
        _out_proj_kernel,
        grid=(rows // tm,),
        in_specs=[
            pl.BlockSpec((tm, A_WIDTH), lambda t: (t, 0)),
            pl.BlockSpec((tm, B_WIDTH), lambda t: (t, 0)),
            pl.BlockSpec((A_WIDTH, D_MODEL), lambda t: (0, 0)),
            pl.BlockSpec((B_WIDTH, D_MODEL), lambda t: (0, 0)),
            pl.BlockSpec((1, D_MODEL), lambda t: (0, 0)),
            pl.BlockSpec((tm, D_MODEL), lambda t: (t, 0)),
        ],
        out_specs=pl.BlockSpec((tm, D_MODEL), lambda t: (t, 0)),
        out_shape=jax.ShapeDtypeStruct(h.shape, h.dtype),
        compiler_params=_cparams(("parallel",)), name="out_proj",
    )(oa, ob, wa, wb, post_g, h)


def _pad_rows(x, rows):
    if x.shape[0] == rows:
        return x
    return jnp.concatenate([x, jnp.zeros((rows - x.shape[0], x.shape[1]), x.dtype)], axis=0)


def _meta_kv(km_ref, vm_ref):
    kmp = _pad_rows(km_ref[...], LANES)
    vmt = _pad_rows(vm_ref[...].astype(jnp.float32), LANES).T.astype(jnp.bfloat16)
    return kmp, vmt


def _diff_attn_kernel(lq1_ref, lk1_ref, lq2_ref, lk2_ref, sg_ref,
                      q_ref, k_ref, km_ref, vt_ref, vm_ref, gate_ref, o_ref,
                      qcat_sc, m_sc, acc_sc, s_sc, mx_sc, sm_sc, mxm_sc, p_sc, alpha_sc,
                      *, n_chunks, tiles_per_chunk, n_units, nq, n_qblocks, n_heads, lambda_init):
    f32 = jnp.float32
    bq = n_units * nq
    ck = tiles_per_chunk * vt_ref.shape[2]
    lam = (jnp.exp(jnp.sum(lq1_ref[...] * lk1_ref[...], axis=-1, keepdims=True))
           - jnp.exp(jnp.sum(lq2_ref[...] * lk2_ref[...], axis=-1, keepdims=True)) + lambda_init)
    sg = sg_ref[...]
    lane = lax.broadcasted_iota(jnp.int32, (1, LANES), 1)
    comp0 = lane < HEAD_DIM
    heads = range(n_heads)

    def cols(hd):
        return slice(hd * LANES, (hd + 1) * LANES)

    kms = [km_ref[:, cols(hd)] for hd in heads]
    vmts = [_pad_rows(vm_ref[:, cols(hd)].astype(f32), LANES).T.astype(jnp.bfloat16) for hd in heads]

    def with_ones(vt):
        return jnp.concatenate([vt, jnp.ones((SUM_ROWS, vt.shape[1]), vt.dtype)], axis=0)

    def block_rows(ref, qb, hd):
        return ref[pl.ds(pl.multiple_of(qb * bq, bq), bq), cols(hd)]

    def start_block(qb, hd):
        qrows = block_rows(q_ref, qb, hd)
        qcats = []
        for u in range(n_units):
            q = _pad_rows(qrows[u * nq:(u + 1) * nq, :], LANES)
            zero = jnp.zeros_like(q)
            qcats += [jnp.where(comp0, q, zero), jnp.where(comp0, zero, q)]
        qcat = jnp.concatenate(qcats, axis=0)
        qcat_sc[hd] = qcat
        s = lax.dot_general(kms[hd], qcat, _NT, preferred_element_type=f32)
        sm_sc[hd] = s
        mxm_sc[hd] = jnp.max(s, axis=0, keepdims=True)
        scores(0, hd)

    def scores(c, hd):
        s = lax.dot_general(k_ref[c * ck:(c + 1) * ck, cols(hd)], qcat_sc[hd], _NT,
                            preferred_element_type=f32)
        s_sc[hd, c % 2] = s
        mx_sc[hd, c % 2] = jnp.max(s, axis=0, keepdims=True)

    def softmax(c, hd):
        m_old = m_sc[hd]
        m_new = jnp.maximum(m_old, mx_sc[hd, c % 2])
        alpha = jnp.exp2(m_old - m_new)
        p_sc[hd, c % 2] = jnp.exp2(s_sc[hd, c % 2] - m_new).astype(jnp.bfloat16)
        alpha_sc[hd, c % 2] = alpha
        m_sc[hd] = m_new

    def pv(c, hd):
        vt = jnp.concatenate([vt_ref[c * tiles_per_chunk + i, cols(hd), :] for i in range(tiles_per_chunk)],
                             axis=1)
        acc_sc[hd] = alpha_sc[hd, c % 2] * acc_sc[hd] + jnp.dot(with_ones(vt), p_sc[hd, c % 2],
                                                                preferred_element_type=f32)

    def finish_block(qb, hd):
        a_all = acc_sc[hd, :B_VDIM, :] * (1.0 / acc_sc[hd, B_VDIM:B_VDIM + 1, :])
        gate = block_rows(gate_ref, qb, hd)
        outs = []
        for u in range(n_units):
            a = a_all[:, u * 2 * LANES:(u + 1) * 2 * LANES]
            o = a[:, :LANES] - lam * a[:, LANES:]
            ms = jnp.mean(o * o, axis=0, keepdims=True)
            y = (o * lax.rsqrt(ms + EPS) * sg) * (1.0 - lambda_init)
            yt = y.T[:nq, :]
            outs.append((yt * gate[u * nq:(u + 1) * nq, :].astype(f32)).astype(o_ref.dtype))
        o_ref[pl.ds(pl.multiple_of(qb * bq, bq), bq), cols(hd)] = jnp.concatenate(outs, axis=0)

    def block(qb, carry):
        for hd in heads:
            m = mxm_sc[hd]
            p = _pad_rows(jnp.exp2(sm_sc[hd] - m).astype(jnp.bfloat16), LANES)
            m_sc[hd] = m
            acc_sc[hd] = jnp.dot(with_ones(vmts[hd]), p, preferred_element_type=f32)
        for c in range(n_chunks):
            if c + 1 < n_chunks:
                for hd in heads:
                    scores(c + 1, hd)
            if c > 0:
                for hd in heads:
                    pv(c - 1, hd)
            for hd in heads:
                softmax(c, hd)
        for hd in heads:
            pv(n_chunks - 1, hd)
        for hd in heads:
            start_block(jnp.minimum(qb + 1, n_qblocks - 1), hd)
        for hd in heads:
            finish_block(qb, hd)
        return carry

    def block_group(i, carry):
        for j in range(blocks_per_iter):
            block(i * blocks_per_iter + j, carry)
        return carry

    blocks_per_iter = math.gcd(DIFF_BLOCKS_PER_ITER, max(n_qblocks // 2, 1))
    for hd in heads:
        start_block(0, hd)
    lax.fori_loop(0, n_qblocks // blocks_per_iter, block_group, 0)


def _diff_attn(lams, sg, q, k, km, vt, vm, gate, *, bsz, seq, tm, q_block, tiles_per_chunk, meta_q,
               lambda_init):
    n_tiles = seq // tm
    tiles_per_chunk = math.gcd(tiles_per_chunk, n_tiles)
    ck = tiles_per_chunk * tm
    n_chunks = n_tiles // tiles_per_chunk
    if meta_q:
        nq, n_units, n_qblocks, n_heads = N_META, 1, 1, B_HEADS
    else:
        nq, n_units, n_qblocks, n_heads = LANES, q_block // LANES, seq // q_block, 1
    q_rows = nq * n_units * n_qblocks
    width = n_units * 2 * LANES
    hw = n_heads * LANES
    kernel = functools.partial(_diff_attn_kernel, n_chunks=n_chunks, tiles_per_chunk=tiles_per_chunk,
                               n_units=n_units, nq=nq, n_qblocks=n_qblocks, n_heads=n_heads,
                               lambda_init=lambda_init)
    lam_spec = pl.BlockSpec((1, HEAD_DIM), lambda b, h: (0, 0))
    bh = lambda b, h: (b, h)
    return pl.pallas_call(
        kernel,
        grid=(bsz, B_HEADS // n_heads),
        in_specs=[
            lam_spec, lam_spec, lam_spec, lam_spec,
            pl.BlockSpec((B_VDIM, 1), lambda b, h: (0, 0)),
            pl.BlockSpec((q_rows, hw), bh),
            pl.BlockSpec((seq, hw), bh),
            pl.BlockSpec((N_META, hw), bh),
            pl.BlockSpec((n_tiles, hw, tm), lambda b, h: (b, h, 0)),
            pl.BlockSpec((N_META, hw), bh),
            pl.BlockSpec((q_rows, hw), bh),
        ],
        out_specs=pl.BlockSpec((q_rows, hw), bh),
        out_shape=jax.ShapeDtypeStruct((bsz * q_rows, B_WIDTH), jnp.bfloat16),
        scratch_shapes=[
            pltpu.VMEM((n_heads, width, LANES), jnp.bfloat16),
            pltpu.VMEM((n_heads, 1, width), jnp.float32),
            pltpu.VMEM((n_heads, B_VDIM + SUM_ROWS, width), jnp.float32),
            pltpu.VMEM((n_heads, 2, ck, width), jnp.float32),
            pltpu.VMEM((n_heads, 2, 1, width), jnp.float32),
            pltpu.VMEM((n_heads, N_META, width), jnp.float32),
            pltpu.VMEM((n_heads, 1, width), jnp.float32),
            pltpu.VMEM((n_heads, 2, ck, width), jnp.bfloat16),
            pltpu.VMEM((n_heads, 2, 1, width), jnp.float32),
        ],
        compiler_params=_cparams(("parallel", "arbitrary")),
        name="diff_attn_meta" if meta_q else "diff_attn",
    )(*lams, sg, q, k, km, vt, vm, gate)


def _window_attn_kernel(sink_ref, q_ref, k_ref, ks_ref, km_ref, kms_ref, vt_ref, vm_ref, gate_ref, o_ref,
                        s_sc, mx_sc, p_sc, m_sc, *, nb, nq, blocks_per_step, meta_q):
    f32 = jnp.float32
    bf = jnp.bfloat16
    step = pl.program_id(1)
    n_keys = s_sc.shape[1]
    n_keys_padded = p_sc.shape[1]
    lane = lax.broadcasted_iota(jnp.int32, (1, LANES), 1)
    lo = lane < HEAD_DIM
    zero = jnp.zeros((LANES, LANES), bf)
    heads = (0, 2, 5, 7, 1, 3, 4, 6)

    km = km_ref[...]
    kms = kms_ref[...]
    _, vmt = _meta_kv(km_ref, vm_ref)
    key_i = lax.broadcasted_iota(jnp.int32, (LANES, LANES), 0)
    qry_i = lax.broadcasted_iota(jnp.int32, (LANES, LANES), 1)
    sink_row = jnp.concatenate([jnp.full((1, LANES), sink_ref[hd] * LOG2E, f32) for hd in heads], axis=1)
    for slot in range(2):
        p_sc[slot, n_keys:, :] = jnp.zeros((n_keys_padded - n_keys, p_sc.shape[2]), bf)

    def rows_of(ref, blk):
        return ref[pl.ds(pl.multiple_of(blk * BLOCK, BLOCK), BLOCK), :]

    def pieces(n):
        if meta_q:
            return ([k_ref[0:BLOCK, :], km], [ks_ref[0:BLOCK, :], kms], [vt_ref[0], vmt],
                    [(N_META + key_i - qry_i) <= WINDOW])
        prev = jnp.maximum(n - 1, 0)
        nxt = jnp.minimum(n + 1, nb - 1)
        return ([rows_of(k_ref, prev), rows_of(k_ref, n), rows_of(k_ref, nxt), km],
                [rows_of(ks_ref, prev), rows_of(ks_ref, n), rows_of(ks_ref, nxt), kms],
                [vt_ref[prev], vt_ref[n], vt_ref[nxt], vmt],
                [(key_i >= qry_i) & (n > 0), None, (key_i <= qry_i) & (n < nb - 1)])

    def scores(j):
        n = step * blocks_per_step + j
        q = _pad_rows(q_ref[j * nq:(j + 1) * nq, :], LANES)
        cols = [q[:, c * LANES:(c + 1) * LANES] for c in range(A_Q // LANES)]
        lo_of = [jnp.where(lo, c, zero) for c in cols]
        hi_of = [jnp.where(lo, zero, c) for c in cols]
        q_plain = jnp.concatenate([lo_of[0], lo_of[1], hi_of[2], hi_of[3]], axis=0)
        q_swap = jnp.concatenate([hi_of[0], hi_of[1], lo_of[2], lo_of[3]], axis=0)
        pk, pks, _, masks = pieces(n)
        s = jnp.concatenate(
            [lax.dot_general(jnp.concatenate(pk, axis=0), q_plain, _NT, preferred_element_type=f32),
             lax.dot_general(jnp.concatenate(pks, axis=0), q_swap, _NT, preferred_element_type=f32)],
            axis=1)
        cols = []
        for t in range(len(heads)):
            sh = s[:, t * LANES:(t + 1) * LANES]
            rows = [sh[r * LANES:(r + 1) * LANES, :] if mask is None
                    else jnp.where(mask, sh[r * LANES:(r + 1) * LANES, :], NEG_INF)
                    for r, mask in enumerate(masks)]
            rows.append(sh[len(masks) * LANES:, :])
            cols.append(jnp.concatenate(rows, axis=0))
        s = jnp.concatenate(cols, axis=1)
        s_sc[j % 2] = s
        mx_sc[j % 2] = jnp.max(s, axis=0, keepdims=True)

    def softmax(j):
        m = jnp.maximum(mx_sc[j % 2], sink_row)
        p_sc[j % 2, :n_keys, :] = jnp.exp2(s_sc[j % 2] - m).astype(bf)
        m_sc[j % 2] = m

    def pv(j):
        n = step * blocks_per_step + j
        _, _, pvt, _ = pieces(n)
        vt = jnp.concatenate(pvt, axis=1)
        vt = jnp.concatenate([vt, jnp.ones((SUM_ROWS, vt.shape[1]), bf)], axis=0)
        ot = jnp.dot(vt, p_sc[j % 2], preferred_element_type=f32)
        l = ot[2 * HEAD_DIM:2 * HEAD_DIM + 1, :] + jnp.exp2(sink_row - m_sc[j % 2])
        ot = ot[:2 * HEAD_DIM, :] * (1.0 / l)
        out_heads = [None] * A_HEADS
        for t, hd in enumerate(heads):
            g = hd // (A_HEADS // A_KV_HEADS)
            out_heads[hd] = ot[g * HEAD_DIM:(g + 1) * HEAD_DIM, t * LANES:(t + 1) * LANES]
        o = jnp.concatenate(out_heads, axis=0).T[:nq, :]
        o_ref[j * nq:(j + 1) * nq, :] = (o * gate_ref[j * nq:(j + 1) * nq, :].astype(f32)).astype(o_ref.dtype)

    scores(0)
    for j in range(blocks_per_step):
        if j > 0:
            pv(j - 1)
        softmax(j)
        if j + 1 < blocks_per_step:
            scores(j + 1)
    pv(blocks_per_step - 1)


def _window_attn(sink, q, k, ks, km, kms, vt, vm, gate, *, bsz, seq, blocks_per_step, meta_q):
    nb = seq // BLOCK
    if meta_q:
        nq, bps, n_steps, n_real_keys = N_META, 1, 1, BLOCK
        k_spec = pl.BlockSpec((BLOCK, A_KV), lambda b, n: (b * nb, 0))
        vt_spec = pl.BlockSpec((1, A_KV, LANES), lambda b, n: (b * nb, 0, 0))
    else:
        bps = math.gcd(blocks_per_step, nb)
        nq, n_steps, n_real_keys = BLOCK, nb // bps, 3 * BLOCK
        k_spec = pl.BlockSpec((seq, A_KV), lambda b, n: (b, 0))
        vt_spec = pl.BlockSpec((nb, A_KV, LANES), lambda b, n: (b, 0, 0))
    q_index = lambda b, n: (b * n_steps + n, 0)
    meta_spec = pl.BlockSpec((N_META, A_KV), lambda b, n: (b, 0))
    return pl.pallas_call(
        functools.partial(_window_attn_kernel, nb=nb, nq=nq, blocks_per_step=bps, meta_q=meta_q),
        grid=(bsz, n_steps),
        in_specs=[
            pl.BlockSpec(memory_space=pltpu.SMEM),
            pl.BlockSpec((bps * nq, A_Q), q_index),
            k_spec, k_spec, meta_spec, meta_spec, vt_spec, meta_spec,
            pl.BlockSpec((bps * nq, A_WIDTH), q_index),
        ],
        out_specs=pl.BlockSpec((bps * nq, A_WIDTH), q_index),
        out_shape=jax.ShapeDtypeStruct((bsz * n_steps * bps * nq, A_WIDTH), jnp.bfloat16),
        scratch_shapes=[
            pltpu.VMEM((2, n_real_keys + N_META, A_HEADS * LANES), jnp.float32),
            pltpu.VMEM((2, 1, A_HEADS * LANES), jnp.float32),
            pltpu.VMEM((2, n_real_keys + LANES, A_HEADS * LANES), jnp.bfloat16),
            pltpu.VMEM((2, 1, A_HEADS * LANES), jnp.float32),
        ],
        compiler_params=_cparams(("parallel", "arbitrary")),
        name="window_attn_meta" if meta_q else "window_attn",
    )(sink, q, k, ks, km, kms, vt, vm, gate)


def _rope_tables(pos):
    inv_freq = 1.0 / (ROPE_THETA ** (jnp.arange(0, HEAD_DIM, 2, dtype=jnp.float32) / HEAD_DIM))
    ang = pos.astype(jnp.float32)[:, None] * inv_freq[None, :]
    ang = jnp.concatenate([ang, ang, ang, ang], axis=-1)
    sign = jnp.where((jnp.arange(LANES) % HEAD_DIM) < HEAD_DIM // 2, -1.0, 1.0).astype(jnp.float32)
    return jnp.cos(ang), jnp.sin(ang) * sign[None, :]


def _encode_both(x_prompt, x_sample, meta_tokens, w_in, w_out, pre_norm_g, post_norm_g, sink_logits,
                 lambda_q1, lambda_k1, lambda_q2, lambda_k2, subln_g, *, tm, q_block):
    d = x_prompt.shape[-1]
    depth = w_in.shape[0]
    assert d == D_MODEL and tm % LANES == 0 and q_block % LANES == 0
    groups = []
    for x in (x_prompt, x_sample):
        bsz, seq, _ = x.shape
        assert seq % tm == 0 and seq % q_block == 0
        groups.append(dict(
            bsz=bsz, seq=seq,
            h_real=x.reshape(bsz * seq, d),
            h_meta=jnp.broadcast_to(meta_tokens[None], (bsz, N_META, d)).reshape(bsz * N_META, d),
            tab_real=_rope_tables(N_META + jnp.arange(seq)),
            tab_meta=_rope_tables(jnp.tile(jnp.arange(N_META), bsz)),
        ))

    for l in range(depth):
        lambda_init = 0.8 - 0.6 * math.exp(-0.3 * l)
        w_i = w_in[l].astype(jnp.bfloat16)
        w_o = w_out[l].astype(jnp.bfloat16)
        pre_g = pre_norm_g[l][None, :]
        post_g = post_norm_g[l][None, :]
        lams = (lambda_q1[l][None, :], lambda_k1[l][None, :], lambda_q2[l][None, :], lambda_k2[l][None, :])
        sg = subln_g[l][:, None]
        for grp in groups:
            bsz, seq = grp["bsz"], grp["seq"]
            real = _proj_call(grp["h_real"], pre_g, w_i, *grp["tab_real"], tm, True, grp.get("prev_real"))
            meta = _proj_call(grp["h_meta"], pre_g, w_i, *grp["tab_meta"], bsz * N_META, False,
                              grp.get("prev_meta"))
            if l > 0:
                grp["h_real"], grp["h_meta"] = real[0], meta[0]
                real, meta = real[1:], meta[1:]
            (aq, ak, aks, avt, ag, bq, bk, bvt, bg) = real
            (maq, mak, maks, mav, mag, mbq, mbk, mbv, mbg) = meta
            win = functools.partial(_window_attn, sink_logits[l], bsz=bsz, seq=seq,
                                    blocks_per_step=WINDOW_BLOCKS_PER_STEP)
            oa = win(aq, ak, aks, mak, maks, avt, mav, ag, meta_q=False)
            moa = win(maq, ak, aks, mak, maks, avt, mav, mag, meta_q=True)
            dif = functools.partial(_diff_attn, lams, sg, bsz=bsz, seq=seq, tm=tm, q_block=q_block,
                                    tiles_per_chunk=DIFF_TILES_PER_CHUNK, lambda_init=lambda_init)
            ob = dif(bq, bk, mbk, bvt, mbv, bg, meta_q=False)
            mob = dif(mbq, bk, mbk, bvt, mbv, mbg, meta_q=True)
            grp["prev_real"] = (oa, ob, w_o[:A_WIDTH], w_o[A_WIDTH:], post_g)
            grp["prev_meta"] = (moa, mob, w_o[:A_WIDTH], w_o[A_WIDTH:], post_g)

    return tuple(_out_proj(*g["prev_real"], g["h_real"], tm).reshape(g["bsz"], g["seq"], d) for g in groups)


def kernel(x_prompt, x_sample, meta_tokens, w_in, w_out, pre_norm_g, post_norm_g, sink_logits,
           lambda_q1, lambda_k1, lambda_q2, lambda_k2, subln_g):
    return _encode_both(x_prompt, x_sample, meta_tokens, w_in, w_out, pre_norm_g, post_norm_g, sink_logits,
                        lambda_q1, lambda_k1, lambda_q2, lambda_k2, subln_g,
                        tm=ROW_TILE, q_block=DIFF_Q_BLOCK)
```
